```python
import math
import jax, jax.numpy as jnp
from jax import lax
import numpy as np

D_MODEL = 2048
BATCH = 16
SEQ = 2048
DEPTH = 1
DEC_BATCH = 32
DEC_SEQ = 1
PAST_LEN = 16384
PAGE_SIZE = 128

H_A = 8
D_A = 64
DV_A = 2 * D_A
H_B = 8
KV_B = 2
DH_B = 128
H_IDX = 16
D_IDX = 64
TOPK = 256
D_FF = 5632
ROPE_THETA = 10000.0
EPS = 1e-6
Q_BLOCK = 128
POOL_NUM, POOL_DEN = 5, 4

COLUMN_WIDTHS = (H_A * 2 * D_A, H_A * 2 * D_A, H_A * DV_A,
                 H_B * DH_B, KV_B * DH_B, KV_B * DH_B,
                 H_IDX * D_IDX, D_IDX, H_IDX)
D_IN = (3 * H_A * 2 * D_A) + H_B * DH_B + 2 * KV_B * DH_B + H_IDX * D_IDX + D_IDX + H_IDX
D_MIX = H_A * DV_A + H_B * DH_B

kernel_name = "hymba_diff_dsa_macaron_step"


def _rmsnorm(x, g):
    x32 = x.astype(jnp.float32)
    y = x32 * lax.rsqrt(jnp.mean(x32 * x32, axis=-1, keepdims=True) + EPS)
    return (y * g.astype(jnp.float32)).astype(x.dtype)


def _rope(x, pos):
    half = x.shape[-1] // 2
    inv = ROPE_THETA ** (-jnp.arange(half, dtype=jnp.float32) / half)
    ang = pos.astype(jnp.float32)[:, None] * inv[None, :]
    cos = jnp.cos(ang)[:, None, :]
    sin = jnp.sin(ang)[:, None, :]
    x1 = x[..., :half].astype(jnp.float32)
    x2 = x[..., half:].astype(jnp.float32)
    return jnp.concatenate([x1 * cos - x2 * sin, x2 * cos + x1 * sin], axis=-1).astype(x.dtype)


def _swiglu(x, w_in, w_out):
    gate, up = jnp.split(x @ w_in, 2, axis=-1)
    return (jax.nn.silu(gate) * up) @ w_out


def _mixer_inputs(h, g, w_in, pos):
    B, S = h.shape[:2]
    z = _rmsnorm(h, g) @ w_in
    offs, acc = [], 0
    for w in COLUMN_WIDTHS[:-1]:
        acc += w
        offs.append(acc)
    zqa, zka, zva, zqb, zkb, zvb, zqi, zki, zwi = jnp.split(z, offs, axis=-1)
    qa = _rope(zqa.reshape(B, S, 2 * H_A, D_A), pos).reshape(B, S, H_A, 2 * D_A)
    ka = _rope(zka.reshape(B, S, 2 * H_A, D_A), pos).reshape(B, S, H_A, 2 * D_A)
    va = zva.reshape(B, S, H_A, DV_A)
    qb = _rope(zqb.reshape(B, S, H_B, DH_B), pos)
    kb = _rope(zkb.reshape(B, S, KV_B, DH_B), pos)
    vb = zvb.reshape(B, S, KV_B, DH_B)
    qi = _rope(zqi.reshape(B, S, H_IDX, D_IDX), pos)
    ki = _rope(zki.reshape(B, S, 1, D_IDX), pos).reshape(B, S, D_IDX)
    return qa, ka, va, qb, kb, vb, qi, ki, zwi


def _mixer_output(oa, ob, g_head, lam_init, w_out):
    B, S = oa.shape[:2]
    oa = _rmsnorm(oa, g_head) * (1.0 - lam_init)
    cat = jnp.concatenate([oa.reshape(B, S, H_A * DV_A), ob.reshape(B, S, H_B * DH_B)], axis=-1)
    return cat @ w_out


def _masked_softmax(s, mask):
    return jax.nn.softmax(jnp.where(mask, s, -jnp.inf), axis=-1)


def _diff_prompt(qa, ka, va, lam):
    B, S = qa.shape[:2]
    nb = S // Q_BLOCK
    scale = D_A ** -0.5
    k1, k2 = ka[..., :D_A], ka[..., D_A:]
    kpos = jnp.arange(S)
    qblocks = qa.reshape(B, nb, Q_BLOCK, H_A, 2 * D_A).swapaxes(0, 1)

    def block(args):
        qblk, start = args
        qpos = start + jnp.arange(Q_BLOCK)
        mask = (kpos[None, :] <= qpos[:, None])[None, None]
        s1 = jnp.einsum('bqhd,bshd->bhqs', qblk[..., :D_A], k1, preferred_element_type=jnp.float32) * scale
        s2 = jnp.einsum('bqhd,bshd->bhqs', qblk[..., D_A:], k2, preferred_element_type=jnp.float32) * scale
        p = _masked_softmax(s1, mask) - lam * _masked_softmax(s2, mask)
        return jnp.einsum('bhqs,bshd->bqhd', p.astype(va.dtype), va)

    out = lax.map(block, (qblocks, jnp.arange(nb) * Q_BLOCK))
    return out.swapaxes(0, 1).reshape(B, S, H_A, DV_A)


def _partial(q, k, v, mask):
    s = jnp.einsum('bthd,bshd->bhts', q, k, preferred_element_type=jnp.float32) * (D_A ** -0.5)
    if mask is not None:
        s = jnp.where(mask, s, -jnp.inf)
    m = jnp.max(s, axis=-1)
    p = jnp.exp(s - m[..., None])
    return m, jnp.sum(p, axis=-1), jnp.einsum('bhts,bshd->bhtd', p, v.astype(jnp.float32))


def _combine(a, b):
    m = jnp.maximum(a[0], b[0])
    ca, cb = jnp.exp(a[0] - m), jnp.exp(b[0] - m)
    return m, a[1] * ca + b[1] * cb, a[2] * ca[..., None] + b[2] * cb[..., None]


def _diff_sample(qa, ka, va, lam, ck, cv, page_table):
    T = qa.shape[1]
    q1, q2 = qa[..., :D_A], qa[..., D_A:]
    causal = jnp.tril(jnp.ones((T, T), dtype=bool))[None, None]
    st1 = _partial(q1, ka[..., :D_A], va, causal)
    st2 = _partial(q2, ka[..., D_A:], va, causal)

    def body(carry, phys):
        kp, vp = ck[phys], cv[phys]
        c1 = _combine(carry[0], _partial(q1, kp[..., :D_A], vp, None))
        c2 = _combine(carry[1], _partial(q2, kp[..., D_A:], vp, None))
        return (c1, c2), None

    (st1, st2), _ = lax.scan(body, (st1, st2), page_table.T)
    o = st1[2] / st1[1][..., None] - lam * (st2[2] / st2[1][..., None])
    return o.transpose(0, 2, 1, 3).astype(va.dtype)


def _indexer_scores(qi, ki, w):
    r = jax.nn.relu(jnp.einsum('bthd,bsd->bths', qi, ki, preferred_element_type=jnp.float32) * (D_IDX ** -0.5))
    return jnp.einsum('bths,bth->bts', r, w.astype(jnp.float32) * (H_IDX ** -0.5))


def _sparse_attend(q, kg, vg, valid):
    B, T = q.shape[:2]
    qg = q.reshape(B, T, KV_B, H_B // KV_B, DH_B)
    s = jnp.einsum('btkgd,btnkd->btkgn', qg, kg, preferred_element_type=jnp.float32) * (DH_B ** -0.5)
    p = _masked_softmax(s, valid[:, :, None, None, :])
    o = jnp.einsum('btkgn,btnkd->btkgd', p.astype(vg.dtype), vg)
    return o.reshape(B, T, H_B, DH_B)


_gather_rows = jax.vmap(lambda rows, idx: rows[idx])


def _dsa_prompt(qb, kb, vb, qi, ki, wi):
    B, S = qb.shape[:2]
    nb = S // Q_BLOCK
    ksel = min(TOPK, S // 4)
    kpos = jnp.arange(S)
    qblk = qb.reshape(B, nb, Q_BLOCK, H_B, DH_B).swapaxes(0, 1)
    qiblk = qi.reshape(B, nb, Q_BLOCK, H_IDX, D_IDX).swapaxes(0, 1)
    wblk = wi.reshape(B, nb, Q_BLOCK, H_IDX).swapaxes(0, 1)

    def block(args):
        q, qib, wb, start = args
        qpos = start + jnp.arange(Q_BLOCK)
        score = jnp.where((kpos[None, :] <= qpos[:, None])[None], _indexer_scores(qib, ki, wb), -jnp.inf)
        _, idx = lax.top_k(score, ksel)
        valid = idx <= qpos[None, :, None]
        return _sparse_attend(q, _gather_rows(kb, idx), _gather_rows(vb, idx), valid)

    out = lax.map(block, (qblk, qiblk, wblk, jnp.arange(nb) * Q_BLOCK))
    return out.swapaxes(0, 1).reshape(B, S, H_B, DH_B)


def _dsa_sample(qb, kb, vb, qi, ki, wi, csk, csv, cik, page_table):
    DB, T = qb.shape[:2]
    past = page_table.shape[1] * PAGE_SIZE
    L = past + T
    ksel = min(TOPK, L // 4)
    ki_all = jnp.concatenate([cik[page_table].reshape(DB, past, D_IDX), ki], axis=1)
    qpos = past + jnp.arange(T)
    kpos = jnp.arange(L)
    score = jnp.where((kpos[None, :] <= qpos[:, None])[None], _indexer_scores(qi, ki_all, wi), -jnp.inf)
    _, idx = lax.top_k(score, ksel)
    valid = idx <= qpos[None, :, None]
    in_past = (idx < past)[..., None, None]
    pidx = jnp.minimum(idx, past - 1)
    phys = _gather_rows(page_table, pidx // PAGE_SIZE)
    slot = pidx % PAGE_SIZE
    nidx = jnp.clip(idx - past, 0, T - 1)
    kg = jnp.where(in_past, csk[phys, slot], _gather_rows(kb, nidx))
    vg = jnp.where(in_past, csv[phys, slot], _gather_rows(vb, nidx))
    return _sparse_attend(qb, kg, vg, valid)


def _layer(h, pos, attend, n1, w1i, w1o, nm, w_in, g_head, w_out, n2, w2i, w2o, lam_init):
    h = h + 0.5 * _swiglu(_rmsnorm(h, n1), w1i, w1o)
    qa, ka, va, qb, kb, vb, qi, ki, wi = _mixer_inputs(h, nm, w_in, pos)
    oa, ob = attend(qa, ka, va, qb, kb, vb, qi, ki, wi)
    h = h + _mixer_output(oa, ob, g_head, lam_init, w_out)
    h = h + 0.5 * _swiglu(_rmsnorm(h, n2), w2i, w2o)
    return h, (ka, va, kb, vb, ki)


def setup_inputs(seed: int = 0) -> dict:
    key = jax.random.key(seed)
    ks = jax.random.split(key, 32)
    n_pages = PAST_LEN // PAGE_SIZE
    n_used = DEC_BATCH * n_pages
    n_phys = n_used * POOL_NUM // POOL_DEN
    f32 = jnp.float32

    def nrm(k, shape, scale=1.0):
        return jax.random.normal(k, shape, f32) * scale

    page_table = jax.random.permutation(ks[0], n_phys)[:n_used].reshape(DEC_BATCH, n_pages).astype(jnp.int32)
    return {
        "x_prompt": nrm(ks[1], (BATCH, SEQ, D_MODEL)),
        "x_sample": nrm(ks[2], (DEC_BATCH, DEC_SEQ, D_MODEL)),
        "cache_diff_k": nrm(ks[3], (DEPTH, n_phys, PAGE_SIZE, H_A, 2 * D_A)),
        "cache_diff_v": nrm(ks[4], (DEPTH, n_phys, PAGE_SIZE, H_A, DV_A)),
        "cache_dsa_k": nrm(ks[5], (DEPTH, n_phys, PAGE_SIZE, KV_B, DH_B)),
        "cache_dsa_v": nrm(ks[6], (DEPTH, n_phys, PAGE_SIZE, KV_B, DH_B)),
        "cache_idx_k": nrm(ks[7], (DEPTH, n_phys, PAGE_SIZE, D_IDX)),
        "page_table": page_table,
        "norm_ffn1": 1.0 + nrm(ks[8], (DEPTH, D_MODEL), 0.01),
        "w_ffn1_in": nrm(ks[9], (DEPTH, D_MODEL, 2 * D_FF), D_MODEL ** -0.5),
        "w_ffn1_out": nrm(ks[10], (DEPTH, D_FF, D_MODEL), D_FF ** -0.5),
        "norm_mix": 1.0 + nrm(ks[11], (DEPTH, D_MODEL), 0.01),
        "w_in": nrm(ks[12], (DEPTH, D_MODEL, D_IN), D_MODEL ** -0.5),
        "lambda_q1": nrm(ks[13], (DEPTH, D_A), 0.1),
        "lambda_k1": nrm(ks[14], (DEPTH, D_A), 0.1),
        "lambda_q2": nrm(ks[15], (DEPTH, D_A), 0.1),
        "lambda_k2": nrm(ks[16], (DEPTH, D_A), 0.1),
        "norm_head": 1.0 + nrm(ks[17], (DEPTH, DV_A), 0.01),
        "w_out": nrm(ks[18], (DEPTH, D_MIX, D_MODEL), D_MIX ** -0.5),
        "norm_ffn2": 1.0 + nrm(ks[19], (DEPTH, D_MODEL), 0.01),
        "w_ffn2_in": nrm(ks[20], (DEPTH, D_MODEL, 2 * D_FF), D_MODEL ** -0.5),
        "w_ffn2_out": nrm(ks[21], (DEPTH, D_FF, D_MODEL), D_FF ** -0.5),
        "norm_final": 1.0 + nrm(ks[22], (D_MODEL,), 0.01),
    }


def reference(x_prompt, x_sample, cache_diff_k, cache_diff_v, cache_dsa_k, cache_dsa_v, cache_idx_k,
              page_table, norm_ffn1, w_ffn1_in, w_ffn1_out, norm_mix, w_in, lambda_q1, lambda_k1,
              lambda_q2, lambda_k2, norm_head, w_out, norm_ffn2, w_ffn2_in, w_ffn2_out, norm_final):
    past = page_table.shape[1] * PAGE_SIZE
    pos_p = jnp.arange(x_prompt.shape[1])
    pos_s = past + jnp.arange(x_sample.shape[1])
    hp, hs = x_prompt, x_sample
    rows_p = ([], [], [], [], [])
    rows_s = ([], [], [], [], [])
    for l in range(DEPTH):
        lam_init = 0.8 - 0.6 * math.exp(-0.3 * l)
        lam = (jnp.exp(jnp.sum(lambda_q1[l].astype(jnp.float32) * lambda_k1[l].astype(jnp.float32)))
               - jnp.exp(jnp.sum(lambda_q2[l].astype(jnp.float32) * lambda_k2[l].astype(jnp.float32)))
               + lam_init)
        lw = (norm_ffn1[l], w_ffn1_in[l], w_ffn1_out[l], norm_mix[l], w_in[l], norm_head[l], w_out[l],
              norm_ffn2[l], w_ffn2_in[l], w_ffn2_out[l], lam_init)

        def attend_prompt(qa, ka, va, qb, kb, vb, qi, ki, wi):
            return _diff_prompt(qa, ka, va, lam), _dsa_prompt(qb, kb, vb, qi, ki, wi)

        def attend_sample(qa, ka, va, qb, kb, vb, qi, ki, wi):
            return (_diff_sample(qa, ka, va, lam, cache_diff_k[l], cache_diff_v[l], page_table),
                    _dsa_sample(qb, kb, vb, qi, ki, wi, cache_dsa_k[l], cache_dsa_v[l], cache_idx_k[l], page_table))

        hp, new_p = _layer(hp, pos_p, attend_prompt, *lw)
        hs, new_s = _layer(hs, pos_s, attend_sample, *lw)
        for lst, r in zip(rows_p, new_p):
            lst.append(r)
        for lst, r in zip(rows_s, new_s):
            lst.append(r)
    y_prompt = _rmsnorm(hp, norm_final)
    y_sample = _rmsnorm(hs, norm_final)
    pdk, pdv, psk, psv, pik = [jnp.stack(r) for r in rows_p]
    sdk, sdv, ssk, ssv, sik = [jnp.stack(r) for r in rows_s]
    return (y_prompt, y_sample, pdk, pdv, psk, psv, pik, sdk, sdv, ssk, ssv, sik)
```

```python
import functools
import math

import jax
import jax.numpy as jnp
from jax import lax
from jax.experimental import pallas as pl
from jax.experimental.pallas import tpu as pltpu

H_A = 8
D_A = 64
DV_A = 2 * D_A
H_B = 8
KV_B = 2
DH_B = 128
H_IDX = 16
D_IDX = 64
TOPK = 256
ROPE_THETA = 10000.0
EPS = 1e-6
PAGE_SIZE = 128

LANES = 128
NEG_BIG = -1e30
INT_MIN = -(2 ** 31)
VMEM_LIMIT = 56 * 1024 * 1024

W_QA = H_A * 2 * D_A
W_KA = H_A * 2 * D_A
W_VA = H_A * DV_A
W_QB = H_B * DH_B
W_KB = KV_B * DH_B
W_VB = KV_B * DH_B
W_QI = H_IDX * D_IDX
W_MAIN = W_QA + W_KA + W_VA + W_QB + W_KB + W_VB + W_QI

F32 = jnp.float32
BF16 = jnp.bfloat16
NT_DIMS = (((1,), (1,)), ((), ()))


def _cparams(sem):
    return pltpu.CompilerParams(dimension_semantics=sem, vmem_limit_bytes=VMEM_LIMIT)


def _rms(x, g):
    return x * lax.rsqrt(jnp.mean(x * x, axis=-1, keepdims=True) + EPS) * g


def _sort_key(x):
    b = lax.bitcast_convert_type(x, jnp.int32)
    return b ^ ((b >> 31) & jnp.int32(0x7FFFFFFF))


def _kth_largest_key(count_ge, like, k):
    def body(it, t):
        c = t | lax.shift_left(jnp.int32(1), 31 - it)
        cnt = count_ge(c ^ jnp.int32(INT_MIN))
        return jnp.where(cnt >= k, c, t)

    t = lax.fori_loop(0, 32, body, jnp.zeros_like(like))
    return t ^ jnp.int32(INT_MIN)


def _tie_index_bound(count_eq_below, need, like, nbits):
    def body(it, jb):
        c = jb | lax.shift_left(jnp.int32(1), nbits - 1 - it)
        return jnp.where(count_eq_below(c) <= need, c, jb)

    return lax.fori_loop(0, nbits, body, jnp.zeros_like(like))


def _ffn_body(x_ref, g_ref, wg_ref, wu_ref, wo_ref, gf_ref, o_ref, xn_ref, acc_ref, *, final_norm):
    j = pl.program_id(1)

    @pl.when(j == 0)
    def _():
        xn_ref[...] = _rms(x_ref[...], g_ref[...]).astype(BF16)
        acc_ref[...] = jnp.zeros_like(acc_ref)

    xn = xn_ref[...]
    gate = jnp.dot(xn, wg_ref[...], preferred_element_type=F32)
    up = jnp.dot(xn, wu_ref[...], preferred_element_type=F32)
    act = (gate * jax.nn.sigmoid(gate) * up).astype(BF16)
    acc_ref[...] += jnp.dot(act, wo_ref[...], preferred_element_type=F32)

    @pl.when(j == pl.num_programs(1) - 1)
    def _():
        h = x_ref[...] + 0.5 * acc_ref[...]
        if final_norm:
            h = _rms(h, gf_ref[...])
        o_ref[...] = h


def _ffn(x, g, w_in, w_out, g_final, *, final_norm):
    m, d = x.shape
    f = w_out.shape[0]
    tm = min(m, 512)
    tf = min(f, 512)
    assert m % tm == 0 and f % tf == 0
    nf = f // tf
    return pl.pallas_call(
        functools.partial(_ffn_body, final_norm=final_norm),
        grid=(m // tm, nf),
        in_specs=[
            pl.BlockSpec((tm, d), lambda i, j: (i, 0)),
            pl.BlockSpec((1, d), lambda i, j: (0, 0)),
            pl.BlockSpec((d, tf), lambda i, j: (0, j)),
            pl.BlockSpec((d, tf), lambda i, j: (0, j + nf)),
            pl.BlockSpec((tf, d), lambda i, j: (j, 0)),
            pl.BlockSpec((1, d), lambda i, j: (0, 0)),
        ],
        out_specs=pl.BlockSpec((tm, d), lambda i, j: (i, 0)),
        out_shape=jax.ShapeDtypeStruct((m, d), F32),
        scratch_shapes=[pltpu.VMEM((tm, d), BF16), pltpu.VMEM((tm, d), F32)],
        compiler_params=_cparams(("parallel", "arbitrary")),
        name="ffn_final" if final_norm else "ffn",
    )(x, g, w_in, w_in, w_out, g_final)


def _inproj_body(x_ref, g_ref, wm_ref, wk_ref, ww_ref, inv64_ref, inv128_ref,
                 qa_ref, ka_ref, kab_ref, va_ref, vab_ref, qb_ref, kb_ref, kbb_ref,
                 vb_ref, vbb_ref, qi_ref, ki_ref, kid_ref, wi_ref,
                 *, tm, pos_offset, period):
    xn = _rms(x_ref[...], g_ref[...]).astype(BF16)
    if period == 1:
        pos = jnp.full((tm, 1), float(pos_offset), F32)
    else:
        row0 = (pl.program_id(0) * tm) % period
        pos = (pos_offset + row0 + lax.broadcasted_iota(jnp.int32, (tm, 1), 0)).astype(F32)
    lane = lax.broadcasted_iota(jnp.int32, (1, LANES), 1)
    lo64 = (lane % 64) < 32
    lo128 = lane < 64
    ang64 = pos * inv64_ref[...]
    ang128 = pos * inv128_ref[...]
    cos64 = jnp.cos(ang64)
    sin64 = jnp.where(lo64, -jnp.sin(ang64), jnp.sin(ang64))
    cos128 = jnp.cos(ang128)
    sin128 = jnp.where(lo128, -jnp.sin(ang128), jnp.sin(ang128))

    def rope64(z):
        rot = jnp.where(lo64, pltpu.roll(z, 96, 1), pltpu.roll(z, 32, 1))
        return z * cos64 + rot * sin64

    def rope128(z):
        return z * cos128 + pltpu.roll(z, 64, 1) * sin128

    def region(w_ref, start, width, fn, outs):
        step = 512 if width % 512 == 0 else width
        for c0 in range(0, width, step):
            z = jnp.dot(xn, w_ref[:, start + c0:start + c0 + step], preferred_element_type=F32)
            for c in range(0, step, LANES):
                zc = fn(z[:, c:c + LANES])
                for ref in outs:
                    ref[:, c0 + c:c0 + c + LANES] = zc.astype(ref.dtype)

    off = 0
    region(wm_ref, off, W_QA, lambda z: rope64(z) * (D_A ** -0.5), [qa_ref]); off += W_QA
    region(wm_ref, off, W_KA, rope64, [ka_ref, kab_ref]); off += W_KA
    region(wm_ref, off, W_VA, lambda z: z, [va_ref, vab_ref]); off += W_VA
    region(wm_ref, off, W_QB, rope128, [qb_ref]); off += W_QB
    region(wm_ref, off, W_KB, rope128, [kb_ref, kbb_ref]); off += W_KB
    region(wm_ref, off, W_VB, lambda z: z, [vb_ref, vbb_ref]); off += W_VB
    region(wm_ref, off, W_QI, lambda z: rope64(z) * (D_IDX ** -0.5), [qi_ref]); off += W_QI
    kd = rope64(jnp.dot(xn, wk_ref[...], preferred_element_type=F32))
    kid_ref[...] = kd.astype(BF16)
    ki_ref[...] = kd[:, :D_IDX]
    wz = jnp.dot(xn, ww_ref[...], preferred_element_type=F32)
    wi_ref[...] = wz[:, :H_IDX] * (H_IDX ** -0.5)


def _inproj(x, g, w_main, w_kidx2, w_widx, inv64, inv128, *, pos_offset, period):
    m, d = x.shape
    tm = min(m, 256)
    assert m % tm == 0 and (period == 1 or period % tm == 0)
    row = lambda w: pl.BlockSpec((tm, w), lambda i: (i, 0))
    const = lambda a: pl.BlockSpec(a.shape, lambda i: (0, 0), pipeline_mode=pl.Buffered(1))
    outs = [
        (W_QA, BF16), (W_KA, F32), (W_KA, BF16), (W_VA, F32), (W_VA, BF16),
        (W_QB, BF16), (W_KB, F32), (W_KB, BF16), (W_VB, F32), (W_VB, BF16),
        (W_QI, BF16), (D_IDX, F32), (2 * D_IDX, BF16), (H_IDX, F32),
    ]
    return pl.pallas_call(
        functools.partial(_inproj_body, tm=tm, pos_offset=pos_offset, period=period),
        grid=(m // tm,),
        in_specs=[row(d), const(g), const(w_main), const(w_kidx2), const(w_widx),
                  const(inv64), const(inv128)],
        out_specs=[row(w) for w, _ in outs],
        out_shape=[jax.ShapeDtypeStruct((m, w), dt) for w, dt in outs],
        compiler_params=_cparams(("parallel",)),
        name="inproj",
    )(x, g, w_main, w_kidx2, w_widx, inv64, inv128)


def _lambda_value(lam_ref, lam_init):
    lp = lam_ref[...]
    a = jnp.sum(lp[0:1] * lp[1:2], axis=-1, keepdims=True)
    b = jnp.sum(lp[2:3] * lp[3:4], axis=-1, keepdims=True)
    return jnp.exp(a) - jnp.exp(b) + lam_init


def _diff_prompt_body(q_ref, k_ref, v_ref, lam_ref, gh_ref, o_ref, *, tq, lam_init):
    i = pl.program_id(2)
    q = q_ref[0]
    lane = lax.broadcasted_iota(jnp.int32, (1, LANES), 1)
    zero = jnp.zeros_like(q)
    qq = jnp.concatenate([jnp.where(lane < D_A, q, zero), jnp.where(lane >= D_A, q, zero)], axis=0)

    def step(j, carry, masked):
        m, l, acc = carry
        start = pl.multiple_of(j * tq, tq)
        k = k_ref[0, pl.ds(start, tq), :]
        v = v_ref[0, pl.ds(start, tq), :]
        s = lax.dot_general(qq, k, NT_DIMS, preferred_element_type=F32)
        if masked:
            r = lax.broadcasted_iota(jnp.int32, (tq, tq), 0)
            c = lax.broadcasted_iota(jnp.int32, (tq, tq), 1)
            bias = jnp.where(c <= r, 0.0, NEG_BIG).astype(F32)
            s = s + jnp.concatenate([bias, bias], axis=0)
        m_new = jnp.maximum(m, jnp.max(s, axis=-1, keepdims=True))
        alpha = jnp.exp(m - m_new)
        p = jnp.exp(s - m_new)
        l = alpha * l + jnp.sum(p, axis=-1, keepdims=True)
        acc = alpha * acc + jnp.dot(p.astype(BF16), v, preferred_element_type=F32)
        return m_new, l, acc

    init = (jnp.full((2 * tq, 1), NEG_BIG, F32), jnp.zeros((2 * tq, 1), F32),
            jnp.zeros((2 * tq, DV_A), F32))
    carry = lax.fori_loop(0, i, functools.partial(step, masked=False), init)
    m, l, acc = step(i, carry, True)
    o = acc / l
    lam = _lambda_value(lam_ref, lam_init)
    o = o[:tq] - lam * o[tq:]
    o_ref[0] = (_rms(o, gh_ref[...]) * (1.0 - lam_init)).astype(o_ref.dtype)


def _diff_prompt(qa, ka, va, lam_params, g_head, *, lam_init):
    b, s, _ = qa.shape
    tq = min(s, 256)
    assert s % tq == 0
    return pl.pallas_call(
        functools.partial(_diff_prompt_body, tq=tq, lam_init=lam_init),
        grid=(b, H_A, s // tq),
        in_specs=[
            pl.BlockSpec((1, tq, DV_A), lambda bb, h, i: (bb, i, h)),
            pl.BlockSpec((1, s, DV_A), lambda bb, h, i: (bb, 0, h)),
            pl.BlockSpec((1, s, DV_A), lambda bb, h, i: (bb, 0, h)),
            pl.BlockSpec(lam_params.shape, lambda bb, h, i: (0, 0)),
            pl.BlockSpec(g_head.shape, lambda bb, h, i: (0, 0)),
        ],
        out_specs=pl.BlockSpec((1, tq, DV_A), lambda bb, h, i: (bb, i, h)),
        out_shape=jax.ShapeDtypeStruct((b, s, H_A * DV_A), BF16),
        compiler_params=_cparams(("parallel", "parallel", "parallel")),
        name="diff_prompt",
    )(qa, ka, va, lam_params, g_head)


def _dsa_prompt_body(qb_ref, qi_ref, wi_ref, kd_ref, kb_ref, vb_ref, o_ref,
                     key_ref, qs_ref, qg_ref, *, tq, tk, topk, idx_bits):
    i = pl.program_id(1)
    nkv = (i * tq + tq - 1) // tk + 1
    lane = lax.broadcasted_iota(jnp.int32, (1, LANES), 1)
    row = lax.broadcasted_iota(jnp.int32, (tq, tk), 0) + i * tq
    col = lax.broadcasted_iota(jnp.int32, (tq, tk), 1)

    for h in range(H_IDX):
        blk = qi_ref[0, :, (h // 2) * LANES:(h // 2 + 1) * LANES]
        keep = (lane >= D_IDX) if h % 2 else (lane < D_IDX)
        qs_ref[h * tq:(h + 1) * tq, :] = jnp.where(keep, blk, jnp.zeros_like(blk))
    w = wi_ref[0]

    def score_block(j, _):
        start = pl.multiple_of(j * tk, tk)
        r = lax.dot_general(qs_ref[...], kd_ref[0, pl.ds(start, tk), :], NT_DIMS,
                            preferred_element_type=F32)
        acc = jnp.zeros((tq, tk), F32)
        for h in range(H_IDX):
            acc = acc + w[:, h:h + 1] * jnp.maximum(r[h * tq:(h + 1) * tq], 0.0)
        acc = jnp.where(col + j * tk <= row, acc, -jnp.inf)
        key_ref[j] = _sort_key(acc)
        return 0

    lax.fori_loop(0, nkv, score_block, 0)

    def row_count(pred_fn):
        def blk(j, part):
            hit = jnp.where(pred_fn(key_ref[j], j), 1.0, 0.0)
            for c in range(0, tk, LANES):
                part = part + hit[:, c:c + LANES]
            return part
        part = lax.fori_loop(0, nkv, blk, jnp.zeros((tq, LANES), F32))
        return jnp.sum(part, axis=-1, keepdims=True).astype(jnp.int32)

    like = jnp.zeros((tq, 1), jnp.int32)
    thr = _kth_largest_key(lambda c: row_count(lambda kk, j: kk >= c), like, topk)
    need = topk - row_count(lambda kk, j: kk > thr)
    jbound = _tie_index_bound(
        lambda c: row_count(lambda kk, j: (kk == thr) & (col + j * tk < c)), need, like, idx_bits)

    scale = DH_B ** -0.5
    group = H_B // KV_B
    for g in range(KV_B):
        for hh in range(group):
            qg_ref[g, hh * tq:(hh + 1) * tq, :] = qb_ref[0, :, (g * group + hh) * DH_B:(g * group + hh + 1) * DH_B]

    def attend_block(j, carry):
        start = pl.multiple_of(j * tk, tk)
        kk = key_ref[j]
        colg = col + j * tk
        sel = ((kk > thr) | ((kk == thr) & (colg < jbound))) & (colg <= row)
        bias1 = jnp.where(sel, 0.0, NEG_BIG).astype(F32)
        bias = jnp.concatenate([bias1] * group, axis=0)
        out = []
        for g in range(KV_B):
            m, l, acc = carry[g]
            kblk = kb_ref[0, pl.ds(start, tk), g * DH_B:(g + 1) * DH_B]
            vblk = vb_ref[0, pl.ds(start, tk), g * DH_B:(g + 1) * DH_B]
            s = lax.dot_general(qg_ref[g], kblk, NT_DIMS, preferred_element_type=F32) * scale + bias
            m_new = jnp.maximum(m, jnp.max(s, axis=-1, keepdims=True))
            alpha = jnp.exp(m - m_new)
            p = jnp.exp(s - m_new)
            l = alpha * l + jnp.sum(p, axis=-1, keepdims=True)
            acc = alpha * acc + jnp.dot(p.astype(BF16), vblk, preferred_element_type=F32)
            out.append((m_new, l, acc))
        return tuple(out)

    rows = group * tq
    init = tuple((jnp.full((rows, 1), NEG_BIG, F32), jnp.zeros((rows, 1), F32),
                  jnp.zeros((rows, DH_B), F32)) for _ in range(KV_B))
    res = lax.fori_loop(0, nkv, attend_block, init)
    for g in range(KV_B):
        _, l, acc = res[g]
        o = acc / l
        for hh in range(group):
            h = g * group + hh
            o_ref[0, :, h * DH_B:(h + 1) * DH_B] = o[hh * tq:(hh + 1) * tq].astype(o_ref.dtype)


def _dsa_prompt(qb, qi, wi, kid, kb, vb):
    b, s, _ = qb.shape
    tq = min(s, 128)
    tk = min(s, 256)
    topk = min(TOPK, s // 4)
    assert s % tq == 0 and s % tk == 0 and tk >= topk and tk % tq == 0
    idx_bits = int(math.ceil(math.log2(s))) + 1
    blk_q = lambda w: pl.BlockSpec((1, tq, w), lambda bb, i: (bb, i, 0))
    blk_s = lambda w: pl.BlockSpec((1, s, w), lambda bb, i: (bb, 0, 0))
    return pl.pallas_call(
        functools.partial(_dsa_prompt_body, tq=tq, tk=tk, topk=topk, idx_bits=idx_bits),
        grid=(b, s // tq),
        in_specs=[blk_q(W_QB), blk_q(W_QI), blk_q(H_IDX), blk_s(2 * D_IDX), blk_s(W_KB), blk_s(W_VB)],
        out_specs=blk_q(W_QB),
        out_shape=jax.ShapeDtypeStruct((b, s, W_QB), BF16),
        scratch_shapes=[
            pltpu.VMEM((s // tk, tq, tk), jnp.int32),
            pltpu.VMEM((H_IDX * tq, LANES), BF16),
            pltpu.VMEM((KV_B, (H_B // KV_B) * tq, DH_B), BF16),
        ],
        compiler_params=_cparams(("parallel", "parallel")),
        name="dsa_prompt",
    )(qb, qi, wi, kid, kb, vb)


def _outproj_body(h_ref, oa_ref, ob_ref, wa_ref, wb_ref, o_ref):
    o_ref[...] = (h_ref[...]
                  + jnp.dot(oa_ref[...], wa_ref[...], preferred_element_type=F32)
                  + jnp.dot(ob_ref[...], wb_ref[...], preferred_element_type=F32))


def _outproj(h, oa, ob, w_a, w_b):
    m, d = h.shape
    tm = min(m, 512)
    assert m % tm == 0
    row = lambda w: pl.BlockSpec((tm, w), lambda i: (i, 0))
    const = lambda a: pl.BlockSpec(a.shape, lambda i: (0, 0), pipeline_mode=pl.Buffered(1))
    return pl.pallas_call(
        _outproj_body,
        grid=(m // tm,),
        in_specs=[row(d), row(oa.shape[1]), row(ob.shape[1]), const(w_a), const(w_b)],
        out_specs=row(d),
        out_shape=jax.ShapeDtypeStruct((m, d), F32),
        compiler_params=_cparams(("parallel",)),
        name="outproj",
    )(h, oa, ob, w_a, w_b)


def _diff_decode_body(pt_ref, q_ref, ks_ref, vs_ref, lam_ref, gh_ref, *rest, pp, lam_init):
    k_refs = rest[:pp]
    v_refs = rest[pp:2 * pp]
    o_ref, m_ref, l_ref, acc_ref = rest[2 * pp:]
    j = pl.program_id(1)
    q = q_ref[0]
    lane = lax.broadcasted_iota(jnp.int32, (1, LANES), 1)
    zero = jnp.zeros_like(q)
    qq = jnp.concatenate([jnp.where(lane < D_A, q, zero), jnp.where(lane >= D_A, q, zero)], axis=0)
    rows_per_page = PAGE_SIZE * H_A

    @pl.when(j == 0)
    def _():
        ks = ks_ref[0].astype(BF16).astype(F32)
        prod = q.astype(F32) * ks
        s1 = jnp.sum(jnp.where(lane < D_A, prod, 0.0), axis=-1, keepdims=True)
        s2 = jnp.sum(jnp.where(lane >= D_A, prod, 0.0), axis=-1, keepdims=True)
        m_ref[...] = jnp.concatenate([s1, s2], axis=0)
        l_ref[...] = jnp.ones_like(l_ref)
        vs = vs_ref[0].astype(BF16).astype(F32)
        acc_ref[...] = jnp.concatenate([vs, vs], axis=0)

    head_of_row = lax.broadcasted_iota(jnp.int32, (2 * H_A, rows_per_page), 0) % H_A
    head_of_lane = lax.broadcasted_iota(jnp.int32, (2 * H_A, rows_per_page), 1) % H_A
    bias = jnp.where(head_of_row == head_of_lane, 0.0, NEG_BIG).astype(F32)
    s_all = []
    for p in range(pp):
        kp = k_refs[p][0].astype(BF16)
        s_all.append(lax.dot_general(qq, kp, NT_DIMS, preferred_element_type=F32) + bias)
    m = m_ref[...]
    m_new = m
    for s in s_all:
        m_new = jnp.maximum(m_new, jnp.max(s, axis=-1, keepdims=True))
    alpha = jnp.exp(m - m_new)
    l = alpha * l_ref[...]
    acc = alpha * acc_ref[...]
    for p in range(pp):
        pr = jnp.exp(s_all[p] - m_new)
        l = l + jnp.sum(pr, axis=-1, keepdims=True)
        acc = acc + jnp.dot(pr.astype(BF16), v_refs[p][0].astype(BF16), preferred_element_type=F32)
    m_ref[...] = m_new
    l_ref[...] = l
    acc_ref[...] = acc

    @pl.when(j == pl.num_programs(1) - 1)
    def _():
        o = acc / l
        lam = _lambda_value(lam_ref, lam_init)
        o = o[:H_A] - lam * o[H_A:]
        o_ref[0] = (_rms(o, gh_ref[...]) * (1.0 - lam_init)).astype(o_ref.dtype)


def _diff_decode(page_table, qa, k_self, v_self, lam_params, g_head, cache_k, cache_v, *, lam_init):
    db, n_pages = page_table.shape
    pp = 4 if n_pages % 4 == 0 else 1
    rows = PAGE_SIZE * H_A
    tok = pl.BlockSpec((1, H_A, DV_A), lambda b, j, pt: (b, 0, 0))
    page = lambda p: pl.BlockSpec((1, rows, DV_A), lambda b, j, pt: (pt[b, j * pp + p], 0, 0))
    grid_spec = pltpu.PrefetchScalarGridSpec(
        num_scalar_prefetch=1,
        grid=(db, n_pages // pp),
        in_specs=[tok, tok, tok,
                  pl.BlockSpec(lam_params.shape, lambda b, j, pt: (0, 0)),
                  pl.BlockSpec(g_head.shape, lambda b, j, pt: (0, 0))]
                 + [page(p) for p in range(pp)] + [page(p) for p in range(pp)],
        out_specs=tok,
        scratch_shapes=[pltpu.VMEM((2 * H_A, 1), F32), pltpu.VMEM((2 * H_A, 1), F32),
                        pltpu.VMEM((2 * H_A, DV_A), F32)],
    )
    return pl.pallas_call(
        functools.partial(_diff_decode_body, pp=pp, lam_init=lam_init),
        grid_spec=grid_spec,
        out_shape=jax.ShapeDtypeStruct((db, H_A, DV_A), BF16),
        compiler_params=_cparams(("parallel", "arbitrary")),
        name="diff_decode",
    )(page_table, qa, k_self, v_self, lam_params, g_head, *([cache_k] * pp), *([cache_v] * pp))


def _idx_decode_body(pt_ref, qs_ref, w_ref, kself_ref, *rest, pp, n_steps):
    k_refs = rest[:pp]
    o_ref = rest[pp]
    j = pl.program_id(1)
    qs = qs_ref[0]
    w = w_ref[0]

    @pl.when(j < n_steps)
    def _():
        for p in range(pp):
            kp = k_refs[p][0].astype(BF16)
            r = lax.dot_general(qs, kp, NT_DIMS, preferred_element_type=F32)
            o_ref[0, p:p + 1, :] = jnp.sum(w * jnp.maximum(r, 0.0), axis=0, keepdims=True)

    @pl.when(j == n_steps)
    def _():
        ks = kself_ref[0].astype(BF16).astype(F32)
        r = jnp.sum(qs.astype(F32) * ks, axis=-1, keepdims=True)
        sc = jnp.sum(w * jnp.maximum(r, 0.0), axis=0, keepdims=True)
        lane = lax.broadcasted_iota(jnp.int32, (1, LANES), 1)
        o_ref[0] = jnp.full((pp, LANES), -jnp.inf, F32)
        o_ref[0, 0:1, :] = jnp.where(lane == 0, sc, -jnp.inf)


def _idx_decode(page_table, qs, w, k_self, cache_idx):
    db, n_pages = page_table.shape
    pp = 8 if n_pages % 8 == 0 else 1
    n_steps = n_pages // pp
    page = lambda p: pl.BlockSpec(
        (1, PAGE_SIZE, D_IDX),
        lambda b, j, pt: (pt[b, jnp.minimum(j * pp + p, n_pages - 1)], 0, 0))
    grid_spec = pltpu.PrefetchScalarGridSpec(
        num_scalar_prefetch=1,
        grid=(db, n_steps + 1),
        in_specs=[pl.BlockSpec((1, H_IDX, D_IDX), lambda b, j, pt: (b, 0, 0)),
                  pl.BlockSpec((1, H_IDX, 1), lambda b, j, pt: (b, 0, 0)),
                  pl.BlockSpec((1, 1, D_IDX), lambda b, j, pt: (b, 0, 0))]
                 + [page(p) for p in range(pp)],
        out_specs=pl.BlockSpec((1, pp, LANES), lambda b, j, pt: (b, j, 0)),
    )
    return pl.pallas_call(
        functools.partial(_idx_decode_body, pp=pp, n_steps=n_steps),
        grid_spec=grid_spec,
        out_shape=jax.ShapeDtypeStruct((db, n_pages + pp, LANES), F32),
        compiler_params=_cparams(("parallel", "arbitrary")),
        name="idx_decode",
    )(page_table, qs, w, k_self, *([cache_idx] * pp))


def _select_decode_body(s_ref, thr_ref, jb_ref, *, topk, idx_bits):
    keys = _sort_key(s_ref[...])
    n = keys.shape[1]
    col = lax.broadcasted_iota(jnp.int32, keys.shape, 1)

    def row_count(pred):
        hit = jnp.where(pred, 1.0, 0.0)
        part = jnp.zeros((keys.shape[0], LANES), F32)
        for c in range(0, n, LANES):
            part = part + hit[:, c:c + LANES]
        return jnp.sum(part, axis=-1, keepdims=True).astype(jnp.int32)

    like = jnp.zeros((keys.shape[0], 1), jnp.int32)
    thr = _kth_largest_key(lambda c: row_count(keys >= c), like, topk)
    need = topk - row_count(keys > thr)
    jb = _tie_index_bound(lambda c: row_count((keys == thr) & (col < c)), need, like, idx_bits)
    thr_ref[...] = thr
    jb_ref[...] = jb


def _select_decode(scores, topk):
    db, n = scores.shape
    idx_bits = int(math.ceil(math.log2(n))) + 1
    return pl.pallas_call(
        functools.partial(_select_decode_body, topk=topk, idx_bits=idx_bits),
        out_shape=[jax.ShapeDtypeStruct((db, 1), jnp.int32)] * 2,
        compiler_params=pltpu.CompilerParams(vmem_limit_bytes=VMEM_LIMIT),
        name="select_decode",
    )(scores)


def _dsa_decode_body(pt_ref, thr_ref, jb_ref, q_ref, ks_ref, vs_ref, sc_ref, *rest, pp, n_steps):
    k_refs = rest[:pp]
    v_refs = rest[pp:2 * pp]
    o_ref, m_ref, l_ref, acc_ref = rest[2 * pp:]
    b = pl.program_id(0)
    j = pl.program_id(1)
    q = q_ref[0]
    thr = thr_ref[b]
    jb = jb_ref[b]
    scale = DH_B ** -0.5
    group = H_B // KV_B
    rows_per_page = PAGE_SIZE * KV_B

    @pl.when(j == 0)
    def _():
        m_ref[...] = jnp.full_like(m_ref, NEG_BIG)
        l_ref[...] = jnp.zeros_like(l_ref)
        acc_ref[...] = jnp.zeros_like(acc_ref)

    keys = _sort_key(sc_ref[0])
    idx = (lax.broadcasted_iota(jnp.int32, (pp, LANES), 0) + j * pp) * PAGE_SIZE \
        + lax.broadcasted_iota(jnp.int32, (pp, LANES), 1)
    sel = (keys > thr) | ((keys == thr) & (idx < jb))

    @pl.when(j < n_steps)
    def _():
        spread = (lax.broadcasted_iota(jnp.int32, (LANES, rows_per_page), 1) // KV_B
                  == lax.broadcasted_iota(jnp.int32, (LANES, rows_per_page), 0))
        sel2 = jnp.dot(jnp.where(sel, 1.0, 0.0).astype(BF16), jnp.where(spread, 1.0, 0.0).astype(BF16),
                       preferred_element_type=F32) > 0.5
        kv_of_row = lax.broadcasted_iota(jnp.int32, (H_B, rows_per_page), 0) // group
        kv_of_lane = lax.broadcasted_iota(jnp.int32, (H_B, rows_per_page), 1) % KV_B
        same_kv = kv_of_row == kv_of_lane
        s_all, ok_all = [], []
        for p in range(pp):
            kp = k_refs[p][0].astype(BF16)
            s = lax.dot_general(q, kp, NT_DIMS, preferred_element_type=F32) * scale
            ok = same_kv & sel2[p:p + 1, :]
            s_all.append(jnp.where(ok, s, NEG_BIG))
            ok_all.append(ok)
        m = m_ref[...]
        m_new = m
        for s in s_all:
            m_new = jnp.maximum(m_new, jnp.max(s, axis=-1, keepdims=True))
        alpha = jnp.exp(m - m_new)
        l = alpha * l_ref[...]
        acc = alpha * acc_ref[...]
        for p in range(pp):
            pr = jnp.where(ok_all[p], jnp.exp(s_all[p] - m_new), 0.0)
            l = l + jnp.sum(pr, axis=-1, keepdims=True)
            acc = acc + jnp.dot(pr.astype(BF16), v_refs[p][0].astype(BF16), preferred_element_type=F32)
        m_ref[...] = m_new
        l_ref[...] = l
        acc_ref[...] = acc

    @pl.when(j == n_steps)
    def _():
        row8 = lax.broadcasted_iota(jnp.int32, (H_B, LANES), 0)
        ks = ks_ref[0].astype(BF16).astype(F32)
        vs = vs_ref[0].astype(BF16).astype(F32)
        kexp = jnp.where(row8 < group, ks[0:1], ks[1:2])
        vexp = jnp.where(row8 < group, vs[0:1], vs[1:2])
        s = jnp.sum(q.astype(F32) * kexp, axis=-1, keepdims=True) * scale
        ok = sel[0:1, 0:1]
        s = jnp.where(ok, s, NEG_BIG)
        m = m_ref[...]
        m_new = jnp.maximum(m, s)
        alpha = jnp.exp(m - m_new)
        pr = jnp.where(ok, jnp.exp(s - m_new), 0.0)
        l = alpha * l_ref[...] + pr
        acc = alpha * acc_ref[...] + pr * vexp
        o_ref[0] = (acc / l).astype(o_ref.dtype)


def _dsa_decode(page_table, thr, jb, qb, k_self, v_self, scores3, cache_k, cache_v):
    db, n_pages = page_table.shape
    pp = scores3.shape[1] - n_pages
    n_steps = n_pages // pp
    rows = PAGE_SIZE * KV_B
    page = lambda p: pl.BlockSpec(
        (1, rows, DH_B),
        lambda b, j, pt, t, jj: (pt[b, jnp.minimum(j * pp + p, n_pages - 1)], 0, 0))
    tok = lambda r: pl.BlockSpec((1, r, DH_B), lambda b, j, pt, t, jj: (b, 0, 0))
    grid_spec = pltpu.PrefetchScalarGridSpec(
        num_scalar_prefetch=3,
        grid=(db, n_steps + 1),
        in_specs=[tok(H_B), tok(KV_B), tok(KV_B),
                  pl.BlockSpec((1, pp, LANES), lambda b, j, pt, t, jj: (b, j, 0))]
                 + [page(p) for p in range(pp)] + [page(p) for p in range(pp)],
        out_specs=tok(H_B),
        scratch_shapes=[pltpu.VMEM((H_B, 1), F32), pltpu.VMEM((H_B, 1), F32),
                        pltpu.VMEM((H_B, DH_B), F32)],
    )
    return pl.pallas_call(
        functools.partial(_dsa_decode_body, pp=pp, n_steps=n_steps, ),
        grid_spec=grid_spec,
        out_shape=jax.ShapeDtypeStruct((db, H_B, DH_B), BF16),
        compiler_params=_cparams(("parallel", "arbitrary")),
        name="dsa_decode",
    )(page_table, thr, jb, qb, k_self, v_self, scores3, *([cache_k] * pp), *([cache_v] * pp))


def _rope_inv(half):
    return ROPE_THETA ** (-jnp.arange(half, dtype=F32) / half)


def kernel(x_prompt, x_sample, cache_diff_k, cache_diff_v, cache_dsa_k, cache_dsa_v, cache_idx_k,
           page_table, norm_ffn1, w_ffn1_in, w_ffn1_out, norm_mix, w_in, lambda_q1, lambda_k1,
           lambda_q2, lambda_k2, norm_head, w_out, norm_ffn2, w_ffn2_in, w_ffn2_out, norm_final):
    depth = w_in.shape[0]
    assert depth == 1 and x_sample.shape[1] == 1
    bsz, seq, d = x_prompt.shape
    db = x_sample.shape[0]
    n_phys = cache_diff_k.shape[1]
    n_pages = page_table.shape[1]
    past = n_pages * PAGE_SIZE
    lam_init = 0.8 - 0.6 * math.exp(-0.3 * 0)

    w1i, w1o = w_ffn1_in[0].astype(BF16), w_ffn1_out[0].astype(BF16)
    w2i, w2o = w_ffn2_in[0].astype(BF16), w_ffn2_out[0].astype(BF16)
    wi_all = w_in[0].astype(BF16)
    w_main = wi_all[:, :W_MAIN]
    w_k = wi_all[:, W_MAIN:W_MAIN + D_IDX]
    w_kidx2 = jnp.concatenate([w_k, w_k], axis=1)
    w_widx = jnp.pad(wi_all[:, W_MAIN + D_IDX:], ((0, 0), (0, LANES - H_IDX)))
    wo_all = w_out[0].astype(BF16)
    wo_a, wo_b = wo_all[:H_A * DV_A], wo_all[H_A * DV_A:]
    g1, gm, g2 = norm_ffn1[0][None], norm_mix[0][None], norm_ffn2[0][None]
    gh, gf = norm_head[0][None], norm_final[None]
    lam_params = jnp.stack([lambda_q1[0], lambda_k1[0], lambda_q2[0], lambda_k2[0]]).astype(F32)
    inv32, inv64h = _rope_inv(D_A // 2), _rope_inv(DH_B // 2)
    inv64 = jnp.tile(inv32, LANES // (D_A // 2))[None]
    inv128 = jnp.tile(inv64h, LANES // (DH_B // 2))[None]

    def front(x2d, pos_offset, period):
        h = _ffn(x2d, g1, w1i, w1o, gf, final_norm=False)
        return h, _inproj(h, gm, w_main, w_kidx2, w_widx, inv64, inv128,
                          pos_offset=pos_offset, period=period)

    def back(h, oa, ob):
        h = _outproj(h, oa, ob, wo_a, wo_b)
        return _ffn(h, g2, w2i, w2o, gf, final_norm=True)

    mp = bsz * seq
    hp, (qa, ka, kab, va, vab, qb, kb, kbb, vb, vbb, qi, ki, kid, wi) = front(
        x_prompt.reshape(mp, d), 0, seq)
    r3 = lambda a: a.reshape(bsz, seq, a.shape[-1])
    oa = _diff_prompt(r3(qa), r3(kab), r3(vab), lam_params, gh, lam_init=lam_init)
    ob = _dsa_prompt(r3(qb), r3(qi), r3(wi), r3(kid), r3(kbb), r3(vbb))
    y_prompt = back(hp, oa.reshape(mp, -1), ob.reshape(mp, -1)).reshape(bsz, seq, d)

    hs, (qa_s, ka_s, _, va_s, _, qb_s, kb_s, _, vb_s, _, qi_s, ki_s, _, wi_s) = front(
        x_sample.reshape(db, d), past, 1)
    oa_s = _diff_decode(
        page_table, qa_s.reshape(db, H_A, DV_A), ka_s.reshape(db, H_A, DV_A),
        va_s.reshape(db, H_A, DV_A), lam_params, gh,
        cache_diff_k.reshape(n_phys, PAGE_SIZE * H_A, DV_A),
        cache_diff_v.reshape(n_phys, PAGE_SIZE * H_A, DV_A), lam_init=lam_init)
    scores3 = _idx_decode(page_table, qi_s.reshape(db, H_IDX, D_IDX), wi_s.reshape(db, H_IDX, 1),
                          ki_s.reshape(db, 1, D_IDX), cache_idx_k.reshape(n_phys, PAGE_SIZE, D_IDX))
    thr, jb = _select_decode(scores3.reshape(db, -1), min(TOPK, (past + 1) // 4))
    ob_s = _dsa_decode(
        page_table, thr.reshape(db), jb.reshape(db), qb_s.reshape(db, H_B, DH_B),
        kb_s.reshape(db, KV_B, DH_B), vb_s.reshape(db, KV_B, DH_B), scores3,
        cache_dsa_k.reshape(n_phys, PAGE_SIZE * KV_B, DH_B),
        cache_dsa_v.reshape(n_phys, PAGE_SIZE * KV_B, DH_B))
    y_sample = back(hs, oa_s.reshape(db, -1), ob_s.reshape(db, -1)).reshape(db, 1, d)

    p5 = lambda a, h, w: a.reshape(1, bsz, seq, h, w)
    s5 = lambda a, h, w: a.reshape(1, db, 1, h, w)
    return (y_prompt, y_sample,
            p5(ka, H_A, 2 * D_A), p5(va, H_A, DV_A), p5(kb, KV_B, DH_B), p5(vb, KV_B, DH_B),
            ki.reshape(1, bsz, seq, D_IDX),
            s5(ka_s, H_A, 2 * D_A), s5(va_s, H_A, DV_A), s5(kb_s, KV_B, DH_B), s5(vb_s, KV_B, DH_B),
            ki_s.reshape(1, db, 1, D_IDX))
```

```python
import functools
import math

import jax
import jax.numpy as jnp
from jax import lax
from jax.experimental import pallas as pl
from jax.experimental.pallas import tpu as pltpu

H_A = 8
D_A = 64
DV_A = 2 * D_A
H_B = 8
KV_B = 2
DH_B = 128
H_IDX = 16
D_IDX = 64
TOPK = 256
ROPE_THETA = 10000.0
EPS = 1e-6
PAGE_SIZE = 128

LANES = 128
KV_TILE = 256
LOG2E = 1.4426950408889634
IDX_PAGES_PER_STEP = 8
NEG_BIG = -1e30
INT_MIN = -(2 ** 31)
VMEM_LIMIT = 56 * 1024 * 1024

W_QA = H_A * 2 * D_A
W_KA = H_A * 2 * D_A
W_VA = H_A * DV_A
W_QB = H_B * DH_B
W_KB = KV_B * DH_B
W_VB = KV_B * DH_B
W_QI = H_IDX * D_IDX
W_MAIN = W_QA + W_KA + W_VA + W_QB + W_KB + W_VB + W_QI

F32 = jnp.float32
BF16 = jnp.bfloat16
NT_DIMS = (((1,), (1,)), ((), ()))


def _cparams(sem):
    return pltpu.CompilerParams(dimension_semantics=sem, vmem_limit_bytes=VMEM_LIMIT)


def _rms(x, g):
    return x * lax.rsqrt(jnp.mean(x * x, axis=-1, keepdims=True) + EPS) * g


def _sort_key(x):
    b = lax.bitcast_convert_type(x, jnp.int32)
    return b ^ ((b >> 31) & jnp.int32(0x7FFFFFFF))


def _kth_largest_key(count_ge, like, k):
    def body(it, t):
        c = t | lax.shift_left(jnp.int32(1), 31 - it)
        cnt = count_ge(c ^ jnp.int32(INT_MIN))
        return jnp.where(cnt >= k, c, t)

    t = lax.fori_loop(0, 32, body, jnp.zeros_like(like))
    return t ^ jnp.int32(INT_MIN)


def _tie_index_bound(count_eq_below, need, like, nbits):
    def body(it, jb):
        c = jb | lax.shift_left(jnp.int32(1), nbits - 1 - it)
        return jnp.where(count_eq_below(c) <= need, c, jb)

    return lax.fori_loop(0, nbits, body, jnp.zeros_like(like))


def _ffn_body(x_ref, g_ref, wg_ref, wu_ref, wo_ref, gf_ref, o_ref, xn_ref, acc_ref, *, final_norm):
    j = pl.program_id(1)

    @pl.when(j == 0)
    def _():
        xn_ref[...] = _rms(x_ref[...], g_ref[...]).astype(BF16)
        acc_ref[...] = jnp.zeros_like(acc_ref)

    xn = xn_ref[...]
    gate = jnp.dot(xn, wg_ref[...], preferred_element_type=F32)
    up = jnp.dot(xn, wu_ref[...], preferred_element_type=F32)
    act = (gate * jax.nn.sigmoid(gate) * up).astype(BF16)
    acc_ref[...] += jnp.dot(act, wo_ref[...], preferred_element_type=F32)

    @pl.when(j == pl.num_programs(1) - 1)
    def _():
        h = x_ref[...] + 0.5 * acc_ref[...]
        if final_norm:
            h = _rms(h, gf_ref[...])
        o_ref[...] = h


def _ffn(x, g, w_in, w_out, g_final, *, final_norm):
    m, d = x.shape
    f = w_out.shape[0]
    tm = min(m, 512)
    tf = min(f, 512)
    assert m % tm == 0 and f % tf == 0
    nf = f // tf
    return pl.pallas_call(
        functools.partial(_ffn_body, final_norm=final_norm),
        grid=(m // tm, nf),
        in_specs=[
            pl.BlockSpec((tm, d), lambda i, j: (i, 0)),
            pl.BlockSpec((1, d), lambda i, j: (0, 0)),
            pl.BlockSpec((d, tf), lambda i, j: (0, j)),
            pl.BlockSpec((d, tf), lambda i, j: (0, j + nf)),
            pl.BlockSpec((tf, d), lambda i, j: (j, 0)),
            pl.BlockSpec((1, d), lambda i, j: (0, 0)),
        ],
        out_specs=pl.BlockSpec((tm, d), lambda i, j: (i, 0)),
        out_shape=jax.ShapeDtypeStruct((m, d), F32),
        scratch_shapes=[pltpu.VMEM((tm, d), BF16), pltpu.VMEM((tm, d), F32)],
        compiler_params=_cparams(("parallel", "arbitrary")),
        name="ffn_final" if final_norm else "ffn",
    )(x, g, w_in, w_in, w_out, g_final)


def _inproj_body(x_ref, g_ref, wm_ref, wk_ref, ww_ref, inv64_ref, inv128_ref, *out_refs,
                 tm, pos_offset, period, prompt):
    if prompt:
        (qa_ref, ka_ref, kab_ref, va_ref, vat_ref, qb_ref, kb_ref, kbb_ref, vb_ref, vbt_ref,
         qi_ref, ki_ref, kid_ref, wi_ref) = out_refs
    else:
        qa_ref, ka_ref, va_ref, qb_ref, kb_ref, vb_ref, qi_ref, ki_ref, wi_ref = out_refs
    xn = _rms(x_ref[...], g_ref[...]).astype(BF16)
    if period == 1:
        pos = jnp.full((tm, 1), float(pos_offset), F32)
    else:
        row0 = (pl.program_id(0) * tm) % period
        pos = (pos_offset + row0 + lax.broadcasted_iota(jnp.int32, (tm, 1), 0)).astype(F32)
    lane = lax.broadcasted_iota(jnp.int32, (1, LANES), 1)
    lo64 = (lane % 64) < 32
    lo128 = lane < 64
    ang64 = pos * inv64_ref[...]
    ang128 = pos * inv128_ref[...]
    cos64 = jnp.cos(ang64)
    sin64 = jnp.where(lo64, -jnp.sin(ang64), jnp.sin(ang64))
    cos128 = jnp.cos(ang128)
    sin128 = jnp.where(lo128, -jnp.sin(ang128), jnp.sin(ang128))

    def rope64(z):
        rot = jnp.where(lo64, pltpu.roll(z, 96, 1), pltpu.roll(z, 32, 1))
        return z * cos64 + rot * sin64

    def rope128(z):
        return z * cos128 + pltpu.roll(z, 64, 1) * sin128

    def rows(ref):
        def put(col, zc):
            ref[:, col:col + LANES] = zc.astype(ref.dtype)
        return put

    def tiles(ref):
        def put(col, zc):
            ref[0, col // LANES] = zc.T.astype(ref.dtype)
        return put

    def region(start, width, fn, sinks):
        step = 512 if width % 512 == 0 else width
        for c0 in range(0, width, step):
            z = jnp.dot(xn, wm_ref[:, start + c0:start + c0 + step], preferred_element_type=F32)
            for c in range(0, step, LANES):
                zc = fn(z[:, c:c + LANES])
                for put in sinks:
                    put(c0 + c, zc)

    ident = lambda z: z
    off = 0
    region(off, W_QA, lambda z: rope64(z) * (D_A ** -0.5), [rows(qa_ref)]); off += W_QA
    region(off, W_KA, rope64, [rows(ka_ref)] + ([rows(kab_ref)] if prompt else [])); off += W_KA
    region(off, W_VA, ident, [rows(va_ref)] + ([tiles(vat_ref)] if prompt else [])); off += W_VA
    region(off, W_QB, rope128, [rows(qb_ref)]); off += W_QB
    region(off, W_KB, rope128, [rows(kb_ref)] + ([rows(kbb_ref)] if prompt else [])); off += W_KB
    region(off, W_VB, ident, [rows(vb_ref)] + ([tiles(vbt_ref)] if prompt else [])); off += W_VB
    region(off, W_QI, lambda z: rope64(z) * (D_IDX ** -0.5), [rows(qi_ref)]); off += W_QI
    kd = rope64(jnp.dot(xn, wk_ref[...], preferred_element_type=F32))
    ki_ref[...] = kd[:, :D_IDX]
    wz = jnp.dot(xn, ww_ref[...], preferred_element_type=F32) * (H_IDX ** -0.5)
    if prompt:
        kid_ref[...] = kd.astype(BF16)
        wi_ref[...] = wz.T[:H_IDX]
    else:
        wi_ref[...] = wz[:, :H_IDX]


def _inproj(x, g, w_main, w_kidx2, w_widx, inv64, inv128, *, pos_offset, period, prompt):
    m, d = x.shape
    tm = min(m, KV_TILE)
    assert m % tm == 0 and (period == 1 or period % tm == 0)
    nt = m // tm
    row = lambda w, dt: (jax.ShapeDtypeStruct((m, w), dt), pl.BlockSpec((tm, w), lambda i: (i, 0)))
    tile = lambda h: (jax.ShapeDtypeStruct((nt, h, LANES, tm), BF16),
                      pl.BlockSpec((1, h, LANES, tm), lambda i: (i, 0, 0, 0)))
    const = lambda a: pl.BlockSpec(a.shape, lambda i: (0, 0), pipeline_mode=pl.Buffered(1))
    if prompt:
        outs = [row(W_QA, BF16), row(W_KA, F32), row(W_KA, BF16), row(W_VA, F32), tile(H_A),
                row(W_QB, BF16), row(W_KB, F32), row(W_KB, BF16), row(W_VB, F32), tile(KV_B),
                row(W_QI, BF16), row(D_IDX, F32), row(2 * D_IDX, BF16),
                (jax.ShapeDtypeStruct((H_IDX, m), F32), pl.BlockSpec((H_IDX, tm), lambda i: (0, i)))]
    else:
        outs = [row(W_QA, BF16), row(W_KA, F32), row(W_VA, F32), row(W_QB, BF16), row(W_KB, F32),
                row(W_VB, F32), row(W_QI, BF16), row(D_IDX, F32), row(H_IDX, F32)]
    return pl.pallas_call(
        functools.partial(_inproj_body, tm=tm, pos_offset=pos_offset, period=period, prompt=prompt),
        grid=(nt,),
        in_specs=[pl.BlockSpec((tm, d), lambda i: (i, 0)), const(g), const(w_main), const(w_kidx2),
                  const(w_widx), const(inv64), const(inv128)],
        out_specs=[o[1] for o in outs],
        out_shape=[o[0] for o in outs],
        compiler_params=_cparams(("parallel",)),
        name="inproj",
    )(x, g, w_main, w_kidx2, w_widx, inv64, inv128)


def _lambda_value(lam_ref, lam_init):
    lp = lam_ref[...]
    a = jnp.sum(lp[0:1] * lp[1:2], axis=-1, keepdims=True)
    b = jnp.sum(lp[2:3] * lp[3:4], axis=-1, keepdims=True)
    return jnp.exp(a) - jnp.exp(b) + lam_init


def _softmax_part_t(s_t, m, l, exp_fn=jnp.exp):
    m_new = jnp.maximum(m, jnp.max(s_t, axis=0, keepdims=True))
    alpha = exp_fn(m - m_new)
    p = exp_fn(s_t - m_new)
    return m_new, alpha * l + jnp.sum(p, axis=0, keepdims=True), alpha, p.astype(BF16)


def _flash_step_t(s_t, v_t, carry, exp_fn=jnp.exp):
    m, l, acc = carry
    m, l, alpha, p = _softmax_part_t(s_t, m, l, exp_fn)
    return m, l, alpha * acc + jnp.dot(v_t, p, preferred_element_type=F32)


def _flash_init(nq, dv):
    return (jnp.full((1, nq), NEG_BIG, F32), jnp.zeros((1, nq), F32), jnp.zeros((dv, nq), F32))


def _diff_prompt_body(q_ref, k_ref, vt_ref, lam_ref, ght_ref, o_ref, *, tq, lam_init):
    i = pl.program_id(2)
    lane = lax.broadcasted_iota(jnp.int32, (1, LANES), 1)
    q = q_ref[0]
    zero = jnp.zeros_like(q)
    qq = jnp.concatenate([jnp.where(lane < D_A, q, zero), jnp.where(lane >= D_A, q, zero)], axis=0)

    def scores(j):
        start = pl.multiple_of(j * tq, tq)
        return lax.dot_general(k_ref[0, pl.ds(start, tq), :], qq, NT_DIMS, preferred_element_type=F32)

    def body(j, carry):
        s_cur, p_prev, a_prev, m, l, acc = carry
        s_next = scores(j + 1)
        acc = a_prev * acc + jnp.dot(vt_ref[jnp.maximum(j - 1, 0), 0], p_prev, preferred_element_type=F32)
        m, l, a_cur, p_cur = _softmax_part_t(s_cur, m, l)
        return s_next, p_cur, a_cur, m, l, acc

    nq = 2 * tq
    init = (scores(0), jnp.zeros((tq, nq), BF16), jnp.ones((1, nq), F32)) + _flash_init(nq, DV_A)
    s_cur, p_prev, a_prev, m, l, acc = lax.fori_loop(0, i, body, init)
    kr = lax.broadcasted_iota(jnp.int32, (tq, tq), 0)
    qc = lax.broadcasted_iota(jnp.int32, (tq, tq), 1)
    bias = jnp.where(kr <= qc, 0.0, NEG_BIG).astype(F32)
    bias = jnp.concatenate([bias, bias], axis=1)
    acc = a_prev * acc + jnp.dot(vt_ref[jnp.maximum(i - 1, 0), 0], p_prev, preferred_element_type=F32)
    m, l, a_cur, p_cur = _softmax_part_t(s_cur + bias, m, l)
    acc = a_cur * acc + jnp.dot(vt_ref[i, 0], p_cur, preferred_element_type=F32)
    o = acc * (1.0 / l)
    lam = _lambda_value(lam_ref, lam_init)
    o = o[:, :tq] - lam * o[:, tq:]
    o = o * lax.rsqrt(jnp.mean(o * o, axis=0, keepdims=True) + EPS) * (ght_ref[...] * (1.0 - lam_init))
    o_ref[0] = o.T.astype(o_ref.dtype)


def _diff_prompt(qa, ka, va_t, lam_params, g_head_t, *, lam_init):
    b, s, _ = qa.shape
    tq = va_t.shape[-1]
    nt = s // tq
    assert s % tq == 0 and va_t.shape == (b * nt, H_A, DV_A, tq)
    return pl.pallas_call(
        functools.partial(_diff_prompt_body, tq=tq, lam_init=lam_init),
        grid=(b, H_A, nt),
        in_specs=[
            pl.BlockSpec((1, tq, DV_A), lambda bb, h, i: (bb, i, h)),
            pl.BlockSpec((1, s, DV_A), lambda bb, h, i: (bb, 0, h)),
            pl.BlockSpec((nt, 1, DV_A, tq), lambda bb, h, i: (bb, h, 0, 0)),
            pl.BlockSpec(lam_params.shape, lambda bb, h, i: (0, 0)),
            pl.BlockSpec(g_head_t.shape, lambda bb, h, i: (0, 0)),
        ],
        out_specs=pl.BlockSpec((1, tq, DV_A), lambda bb, h, i: (bb, i, h)),
        out_shape=jax.ShapeDtypeStruct((b, s, H_A * DV_A), BF16),
        compiler_params=_cparams(("parallel", "parallel", "parallel")),
        name="diff_prompt",
    )(qa, ka, va_t, lam_params, g_head_t)


def _dsa_prompt_body(qb_ref, qi_ref, wit_ref, kd_ref, kb_ref, vbt_ref, o_ref,
                     key_ref, qs_ref, qg_ref, *, tq, tk, topk, idx_bits):
    i = pl.program_id(1)
    nkv = (i * tq + tq - 1) // tk + 1
    lane = lax.broadcasted_iota(jnp.int32, (1, LANES), 1)
    krow = lax.broadcasted_iota(jnp.int32, (tk, tq), 0)
    qcol = lax.broadcasted_iota(jnp.int32, (tk, tq), 1) + i * tq

    for h in range(H_IDX):
        blk = qi_ref[0, :, (h // 2) * LANES:(h // 2 + 1) * LANES]
        keep = (lane >= D_IDX) if h % 2 else (lane < D_IDX)
        qs_ref[h * tq:(h + 1) * tq, :] = jnp.where(keep, blk, jnp.zeros_like(blk))
    wt = wit_ref[...]

    def score_block(j, _):
        start = pl.multiple_of(j * tk, tk)
        r = lax.dot_general(kd_ref[0, pl.ds(start, tk), :], qs_ref[...], NT_DIMS,
                            preferred_element_type=F32)
        acc = jnp.zeros((tk, tq), F32)
        for h in range(H_IDX):
            acc = acc + wt[h:h + 1, :] * jnp.maximum(r[:, h * tq:(h + 1) * tq], 0.0)
        acc = jnp.where(krow + j * tk <= qcol, acc, -jnp.inf)
        key_ref[j] = _sort_key(acc)
        return 0

    lax.fori_loop(0, nkv, score_block, 0)

    def count(pred_fn):
        def blk(j, part):
            hit = jnp.where(pred_fn(key_ref[j], j), 1.0, 0.0)
            parts = [hit[r:r + 8] for r in range(0, tk, 8)]
            while len(parts) > 1:
                parts = [parts[a] + parts[a + 1] for a in range(0, len(parts), 2)]
            return part + parts[0]
        part = lax.fori_loop(0, nkv, blk, jnp.zeros((8, tq), F32))
        return jnp.sum(part, axis=0, keepdims=True).astype(jnp.int32)

    like = jnp.zeros((1, tq), jnp.int32)
    thr = _kth_largest_key(lambda c: count(lambda kk, j: kk >= c), like, topk)
    need = topk - count(lambda kk, j: kk > thr)
    n_eq = count(lambda kk, j: kk == thr)
    has_ties = jnp.max(jnp.where(n_eq != need, 1.0, 0.0)) > 0.5
    jbound = lax.cond(
        has_ties,
        lambda: _tie_index_bound(
            lambda c: count(lambda kk, j: (kk == thr) & (krow + j * tk < c)), need, like, idx_bits),
        lambda: jnp.full((1, tq), 2 ** idx_bits, jnp.int32))

    scale2 = (DH_B ** -0.5) * LOG2E
    group = H_B // KV_B
    for g in range(KV_B):
        for hh in range(group):
            qg_ref[g, hh * tq:(hh + 1) * tq, :] = qb_ref[0, :, (g * group + hh) * DH_B:(g * group + hh + 1) * DH_B]

    def attend_block(j, carry):
        start = pl.multiple_of(j * tk, tk)
        kk = key_ref[j]
        kidx = krow + j * tk
        sel = ((kk > thr) | ((kk == thr) & (kidx < jbound))) & (kidx <= qcol)
        bias1 = jnp.where(sel, 0.0, NEG_BIG).astype(F32)
        bias = jnp.concatenate([bias1] * group, axis=1)
        out = []
        for g in range(KV_B):
            s = lax.dot_general(kb_ref[0, pl.ds(start, tk), g * DH_B:(g + 1) * DH_B], qg_ref[g], NT_DIMS,
                                preferred_element_type=F32) * scale2 + bias
            out.append(_flash_step_t(s, vbt_ref[j, g], carry[g], exp_fn=jnp.exp2))
        return tuple(out)

    init = tuple(_flash_init(group * tq, DH_B) for _ in range(KV_B))
    res = lax.fori_loop(0, nkv, attend_block, init)
    for g in range(KV_B):
        _, l, acc = res[g]
        o = acc * (1.0 / l)
        for hh in range(group):
            h = g * group + hh
            o_ref[0, :, h * DH_B:(h + 1) * DH_B] = o[:, hh * tq:(hh + 1) * tq].T.astype(o_ref.dtype)


def _dsa_prompt(qb, qi, wi_t, kid, kb, vb_t):
    b, s, _ = qb.shape
    tq = min(s, LANES)
    tk = vb_t.shape[-1]
    nt = s // tk
    topk = min(TOPK, s // 4)
    assert s % tq == 0 and s % tk == 0 and tk >= topk and tk % tq == 0
    assert vb_t.shape == (b * nt, KV_B, DH_B, tk) and wi_t.shape == (H_IDX, b * s)
    idx_bits = int(math.ceil(math.log2(s))) + 1
    blk_q = lambda w: pl.BlockSpec((1, tq, w), lambda bb, i: (bb, i, 0))
    blk_s = lambda w: pl.BlockSpec((1, s, w), lambda bb, i: (bb, 0, 0))
    return pl.pallas_call(
        functools.partial(_dsa_prompt_body, tq=tq, tk=tk, topk=topk, idx_bits=idx_bits),
        grid=(b, s // tq),
        in_specs=[blk_q(W_QB), blk_q(W_QI),
                  pl.BlockSpec((H_IDX, tq), lambda bb, i: (0, bb * (s // tq) + i)),
                  blk_s(2 * D_IDX), blk_s(W_KB),
                  pl.BlockSpec((nt, KV_B, DH_B, tk), lambda bb, i: (bb, 0, 0, 0))],
        out_specs=blk_q(W_QB),
        out_shape=jax.ShapeDtypeStruct((b, s, W_QB), BF16),
        scratch_shapes=[
            pltpu.VMEM((nt, tk, tq), jnp.int32),
            pltpu.VMEM((H_IDX * tq, LANES), BF16),
            pltpu.VMEM((KV_B, (H_B // KV_B) * tq, DH_B), BF16),
        ],
        compiler_params=_cparams(("parallel", "parallel")),
        name="dsa_prompt",
    )(qb, qi, wi_t, kid, kb, vb_t)


def _outproj_body(h_ref, oa_ref, ob_ref, wa_ref, wb_ref, o_ref):
    o_ref[...] = (h_ref[...]
                  + jnp.dot(oa_ref[...], wa_ref[...], preferred_element_type=F32)
                  + jnp.dot(ob_ref[...], wb_ref[...], preferred_element_type=F32))


def _outproj(h, oa, ob, w_a, w_b):
    m, d = h.shape
    tm = min(m, 512)
    assert m % tm == 0
    row = lambda w: pl.BlockSpec((tm, w), lambda i: (i, 0))
    const = lambda a: pl.BlockSpec(a.shape, lambda i: (0, 0), pipeline_mode=pl.Buffered(1))
    return pl.pallas_call(
        _outproj_body,
        grid=(m // tm,),
        in_specs=[row(d), row(oa.shape[1]), row(ob.shape[1]), const(w_a), const(w_b)],
        out_specs=row(d),
        out_shape=jax.ShapeDtypeStruct((m, d), F32),
        compiler_params=_cparams(("parallel",)),
        name="outproj",
    )(h, oa, ob, w_a, w_b)


def _diff_decode_body(pt_ref, q_ref, ks_ref, vs_ref, lam_ref, gh_ref, *rest, pp, lam_init):
    k_refs = rest[:pp]
    v_refs = rest[pp:2 * pp]
    o_ref, m_ref, l_ref, acc_ref = rest[2 * pp:]
    j = pl.program_id(1)
    q = q_ref[0]
    lane = lax.broadcasted_iota(jnp.int32, (1, LANES), 1)
    zero = jnp.zeros_like(q)
    qq = jnp.concatenate([jnp.where(lane < D_A, q, zero), jnp.where(lane >= D_A, q, zero)], axis=0)
    rows_per_page = PAGE_SIZE * H_A

    @pl.when(j == 0)
    def _():
        ks = ks_ref[0].astype(BF16).astype(F32)
        prod = q.astype(F32) * ks
        s1 = jnp.sum(jnp.where(lane < D_A, prod, 0.0), axis=-1, keepdims=True)
        s2 = jnp.sum(jnp.where(lane >= D_A, prod, 0.0), axis=-1, keepdims=True)
        m_ref[...] = jnp.concatenate([s1, s2], axis=0)
        l_ref[...] = jnp.ones_like(l_ref)
        vs = vs_ref[0].astype(BF16).astype(F32)
        acc_ref[...] = jnp.concatenate([vs, vs], axis=0)

    head_of_row = lax.broadcasted_iota(jnp.int32, (2 * H_A, rows_per_page), 0) % H_A
    head_of_lane = lax.broadcasted_iota(jnp.int32, (2 * H_A, rows_per_page), 1) % H_A
    bias = jnp.where(head_of_row == head_of_lane, 0.0, NEG_BIG).astype(F32)
    s_all = []
    for p in range(pp):
        kp = k_refs[p][0].astype(BF16)
        s_all.append(lax.dot_general(qq, kp, NT_DIMS, preferred_element_type=F32) + bias)
    m = m_ref[...]
    m_new = m
    for s in s_all:
        m_new = jnp.maximum(m_new, jnp.max(s, axis=-1, keepdims=True))
    alpha = jnp.exp(m - m_new)
    l = alpha * l_ref[...]
    acc = alpha * acc_ref[...]
    for p in range(pp):
        pr = jnp.exp(s_all[p] - m_new)
        l = l + jnp.sum(pr, axis=-1, keepdims=True)
        acc = acc + jnp.dot(pr.astype(BF16), v_refs[p][0].astype(BF16), preferred_element_type=F32)
    m_ref[...] = m_new
    l_ref[...] = l
    acc_ref[...] = acc

    @pl.when(j == pl.num_programs(1) - 1)
    def _():
        o = acc / l
        lam = _lambda_value(lam_ref, lam_init)
        o = o[:H_A] - lam * o[H_A:]
        o_ref[0] = (_rms(o, gh_ref[...]) * (1.0 - lam_init)).astype(o_ref.dtype)


def _diff_decode(page_table, qa, k_self, v_self, lam_params, g_head, cache_k, cache_v, *, lam_init):
    db, n_pages = page_table.shape
    pp = 8 if n_pages % 8 == 0 else 1
    rows = PAGE_SIZE * H_A
    tok = pl.BlockSpec((1, H_A, DV_A), lambda b, j, pt: (b, 0, 0))
    page = lambda p: pl.BlockSpec((1, rows, DV_A), lambda b, j, pt: (pt[b, j * pp + p], 0, 0))
    grid_spec = pltpu.PrefetchScalarGridSpec(
        num_scalar_prefetch=1,
        grid=(db, n_pages // pp),
        in_specs=[tok, tok, tok,
                  pl.BlockSpec(lam_params.shape, lambda b, j, pt: (0, 0)),
                  pl.BlockSpec(g_head.shape, lambda b, j, pt: (0, 0))]
                 + [page(p) for p in range(pp)] + [page(p) for p in range(pp)],
        out_specs=tok,
        scratch_shapes=[pltpu.VMEM((2 * H_A, 1), F32), pltpu.VMEM((2 * H_A, 1), F32),
                        pltpu.VMEM((2 * H_A, DV_A), F32)],
    )
    return pl.pallas_call(
        functools.partial(_diff_decode_body, pp=pp, lam_init=lam_init),
        grid_spec=grid_spec,
        out_shape=jax.ShapeDtypeStruct((db, H_A, DV_A), BF16),
        compiler_params=_cparams(("parallel", "arbitrary")),
        name="diff_decode",
    )(page_table, qa, k_self, v_self, lam_params, g_head, *([cache_k] * pp), *([cache_v] * pp))


def _idx_decode_body(pt_ref, qs_ref, w_ref, kself_ref, *rest, pp, n_steps):
    k_refs = rest[:pp]
    o_ref = rest[pp]
    j = pl.program_id(1)
    qs = qs_ref[0]
    w = w_ref[0]

    @pl.when(j < n_steps)
    def _():
        for p in range(pp):
            kp = k_refs[p][0].astype(BF16)
            r = jnp.dot(qs, kp, preferred_element_type=F32)
            o_ref[0, p:p + 1, :] = jnp.sum(w * jnp.maximum(r, 0.0), axis=0, keepdims=True)

    @pl.when(j == n_steps)
    def _():
        ks = kself_ref[0].astype(BF16).astype(F32)
        r = jnp.sum(qs.astype(F32) * ks, axis=-1, keepdims=True)
        sc = jnp.sum(w * jnp.maximum(r, 0.0), axis=0, keepdims=True)
        lane = lax.broadcasted_iota(jnp.int32, (1, LANES), 1)
        o_ref[0] = jnp.full((pp, LANES), -jnp.inf, F32)
        o_ref[0, 0:1, :] = jnp.where(lane == 0, sc, -jnp.inf)


def _idx_decode(page_table, qs, w, k_self, cache_idx_t):
    db, n_pages = page_table.shape
    pp = IDX_PAGES_PER_STEP
    assert n_pages % pp == 0
    n_steps = n_pages // pp
    page = lambda p: pl.BlockSpec(
        (1, D_IDX, PAGE_SIZE),
        lambda b, j, pt: (pt[b, jnp.minimum(j * pp + p, n_pages - 1)], 0, 0))
    grid_spec = pltpu.PrefetchScalarGridSpec(
        num_scalar_prefetch=1,
        grid=(db, n_steps + 1),
        in_specs=[pl.BlockSpec((1, H_IDX, D_IDX), lambda b, j, pt: (b, 0, 0)),
                  pl.BlockSpec((1, H_IDX, 1), lambda b, j, pt: (b, 0, 0)),
                  pl.BlockSpec((1, 1, D_IDX), lambda b, j, pt: (b, 0, 0))]
                 + [page(p) for p in range(pp)],
        out_specs=pl.BlockSpec((1, pp, LANES), lambda b, j, pt: (b, j, 0)),
    )
    return pl.pallas_call(
        functools.partial(_idx_decode_body, pp=pp, n_steps=n_steps),
        grid_spec=grid_spec,
        out_shape=jax.ShapeDtypeStruct((db, n_pages + pp, LANES), F32),
        compiler_params=_cparams(("parallel", "arbitrary")),
        name="idx_decode",
    )(page_table, qs, w, k_self, *([cache_idx_t] * pp))


def _select_decode_body(s_ref, thr_ref, jb_ref, *, topk, idx_bits):
    keys = _sort_key(s_ref[...])
    n = keys.shape[1]
    col = lax.broadcasted_iota(jnp.int32, keys.shape, 1)

    def row_count(pred):
        hit = jnp.where(pred, 1.0, 0.0)
        part = jnp.zeros((keys.shape[0], LANES), F32)
        for c in range(0, n, LANES):
            part = part + hit[:, c:c + LANES]
        return jnp.sum(part, axis=-1, keepdims=True).astype(jnp.int32)

    like = jnp.zeros((keys.shape[0], 1), jnp.int32)
    thr = _kth_largest_key(lambda c: row_count(keys >= c), like, topk)
    need = topk - row_count(keys > thr)
    jb = _tie_index_bound(lambda c: row_count((keys == thr) & (col < c)), need, like, idx_bits)
    thr_ref[...] = thr
    jb_ref[...] = jb


def _select_decode(scores, topk):
    db, n = scores.shape
    idx_bits = int(math.ceil(math.log2(n))) + 1
    return pl.pallas_call(
        functools.partial(_select_decode_body, topk=topk, idx_bits=idx_bits),
        out_shape=[jax.ShapeDtypeStruct((db, 1), jnp.int32)] * 2,
        compiler_params=pltpu.CompilerParams(vmem_limit_bytes=VMEM_LIMIT),
        name="select_decode",
    )(scores)


def _dsa_decode_body(pt_ref, thr_ref, jb_ref, q_ref, ks_ref, vs_ref, sc_ref, *rest, pp, n_steps):
    k_refs = rest[:pp]
    v_refs = rest[pp:2 * pp]
    o_ref, m_ref, l_ref, acc_ref = rest[2 * pp:]
    b = pl.program_id(0)
    j = pl.program_id(1)
    q = q_ref[0]
    thr = thr_ref[b]
    jb = jb_ref[b]
    scale = DH_B ** -0.5
    group = H_B // KV_B
    rows_per_page = PAGE_SIZE * KV_B

    @pl.when(j == 0)
    def _():
        m_ref[...] = jnp.full_like(m_ref, NEG_BIG)
        l_ref[...] = jnp.zeros_like(l_ref)
        acc_ref[...] = jnp.zeros_like(acc_ref)

    keys = _sort_key(sc_ref[0])
    idx = (lax.broadcasted_iota(jnp.int32, (pp, LANES), 0) + j * pp) * PAGE_SIZE \
        + lax.broadcasted_iota(jnp.int32, (pp, LANES), 1)
    sel = (keys > thr) | ((keys == thr) & (idx < jb))

    @pl.when(j < n_steps)
    def _():
        spread = (lax.broadcasted_iota(jnp.int32, (LANES, rows_per_page), 1) // KV_B
                  == lax.broadcasted_iota(jnp.int32, (LANES, rows_per_page), 0))
        sel2 = jnp.dot(jnp.where(sel, 1.0, 0.0).astype(BF16), jnp.where(spread, 1.0, 0.0).astype(BF16),
                       preferred_element_type=F32) > 0.5
        kv_of_row = lax.broadcasted_iota(jnp.int32, (H_B, rows_per_page), 0) // group
        kv_of_lane = lax.broadcasted_iota(jnp.int32, (H_B, rows_per_page), 1) % KV_B
        same_kv = kv_of_row == kv_of_lane
        s_all, ok_all = [], []
        for p in range(pp):
            kp = k_refs[p][0].astype(BF16)
            s = lax.dot_general(q, kp, NT_DIMS, preferred_element_type=F32) * scale
            ok = same_kv & sel2[p:p + 1, :]
            s_all.append(jnp.where(ok, s, NEG_BIG))
            ok_all.append(ok)
        m = m_ref[...]
        m_new = m
        for s in s_all:
            m_new = jnp.maximum(m_new, jnp.max(s, axis=-1, keepdims=True))
        alpha = jnp.exp(m - m_new)
        l = alpha * l_ref[...]
        acc = alpha * acc_ref[...]
        for p in range(pp):
            pr = jnp.where(ok_all[p], jnp.exp(s_all[p] - m_new), 0.0)
            l = l + jnp.sum(pr, axis=-1, keepdims=True)
            acc = acc + jnp.dot(pr.astype(BF16), v_refs[p][0].astype(BF16), preferred_element_type=F32)
        m_ref[...] = m_new
        l_ref[...] = l
        acc_ref[...] = acc

    @pl.when(j == n_steps)
    def _():
        row8 = lax.broadcasted_iota(jnp.int32, (H_B, LANES), 0)
        ks = ks_ref[0].astype(BF16).astype(F32)
        vs = vs_ref[0].astype(BF16).astype(F32)
        kexp = jnp.where(row8 < group, ks[0:1], ks[1:2])
        vexp = jnp.where(row8 < group, vs[0:1], vs[1:2])
        s = jnp.sum(q.astype(F32) * kexp, axis=-1, keepdims=True) * scale
        ok = sel[0:1, 0:1]
        s = jnp.where(ok, s, NEG_BIG)
        m = m_ref[...]
        m_new = jnp.maximum(m, s)
        alpha = jnp.exp(m - m_new)
        pr = jnp.where(ok, jnp.exp(s - m_new), 0.0)
        l = alpha * l_ref[...] + pr
        acc = alpha * acc_ref[...] + pr * vexp
        o_ref[0] = (acc / l).astype(o_ref.dtype)


def _dsa_decode(page_table, thr, jb, qb, k_self, v_self, scores3, cache_k, cache_v):
    db, n_pages = page_table.shape
    pp = scores3.shape[1] - n_pages
    n_steps = n_pages // pp
    rows = PAGE_SIZE * KV_B
    page = lambda p: pl.BlockSpec(
        (1, rows, DH_B),
        lambda b, j, pt, t, jj: (pt[b, jnp.minimum(j * pp + p, n_pages - 1)], 0, 0))
    tok = lambda r: pl.BlockSpec((1, r, DH_B), lambda b, j, pt, t, jj: (b, 0, 0))
    grid_spec = pltpu.PrefetchScalarGridSpec(
        num_scalar_prefetch=3,
        grid=(db, n_steps + 1),
        in_specs=[tok(H_B), tok(KV_B), tok(KV_B),
                  pl.BlockSpec((1, pp, LANES), lambda b, j, pt, t, jj: (b, j, 0))]
                 + [page(p) for p in range(pp)] + [page(p) for p in range(pp)],
        out_specs=tok(H_B),
        scratch_shapes=[pltpu.VMEM((H_B, 1), F32), pltpu.VMEM((H_B, 1), F32),
                        pltpu.VMEM((H_B, DH_B), F32)],
    )
    return pl.pallas_call(
        functools.partial(_dsa_decode_body, pp=pp, n_steps=n_steps),
        grid_spec=grid_spec,
        out_shape=jax.ShapeDtypeStruct((db, H_B, DH_B), BF16),
        compiler_params=_cparams(("parallel", "arbitrary")),
        name="dsa_decode",
    )(page_table, thr, jb, qb, k_self, v_self, scores3, *([cache_k] * pp), *([cache_v] * pp))


def _rope_inv(half):
    return ROPE_THETA ** (-jnp.arange(half, dtype=F32) / half)


def kernel(x_prompt, x_sample, cache_diff_k, cache_diff_v, cache_dsa_k, cache_dsa_v, cache_idx_k,
           page_table, norm_ffn1, w_ffn1_in, w_ffn1_out, norm_mix, w_in, lambda_q1, lambda_k1,
           lambda_q2, lambda_k2, norm_head, w_out, norm_ffn2, w_ffn2_in, w_ffn2_out, norm_final):
    depth = w_in.shape[0]
    assert depth == 1 and x_sample.shape[1] == 1
    bsz, seq, d = x_prompt.shape
    db = x_sample.shape[0]
    n_phys = cache_diff_k.shape[1]
    n_pages = page_table.shape[1]
    past = n_pages * PAGE_SIZE
    lam_init = 0.8 - 0.6 * math.exp(-0.3 * 0)

    w1i, w1o = w_ffn1_in[0].astype(BF16), w_ffn1_out[0].astype(BF16)
    w2i, w2o = w_ffn2_in[0].astype(BF16), w_ffn2_out[0].astype(BF16)
    wi_all = w_in[0].astype(BF16)
    w_main = wi_all[:, :W_MAIN]
    w_k = wi_all[:, W_MAIN:W_MAIN + D_IDX]
    w_kidx2 = jnp.concatenate([w_k, w_k], axis=1)
    w_widx = jnp.pad(wi_all[:, W_MAIN + D_IDX:], ((0, 0), (0, LANES - H_IDX)))
    wo_all = w_out[0].astype(BF16)
    wo_a, wo_b = wo_all[:H_A * DV_A], wo_all[H_A * DV_A:]
    g1, gm, g2 = norm_ffn1[0][None], norm_mix[0][None], norm_ffn2[0][None]
    gh, gf = norm_head[0][None], norm_final[None]
    lam_params = jnp.stack([lambda_q1[0], lambda_k1[0], lambda_q2[0], lambda_k2[0]]).astype(F32)
    inv32, inv64h = _rope_inv(D_A // 2), _rope_inv(DH_B // 2)
    inv64 = jnp.tile(inv32, LANES // (D_A // 2))[None]
    inv128 = jnp.tile(inv64h, LANES // (DH_B // 2))[None]

    def front(x2d, pos_offset, period, prompt):
        h = _ffn(x2d, g1, w1i, w1o, gf, final_norm=False)
        return h, _inproj(h, gm, w_main, w_kidx2, w_widx, inv64, inv128,
                          pos_offset=pos_offset, period=period, prompt=prompt)

    def back(h, oa, ob):
        h = _outproj(h, oa, ob, wo_a, wo_b)
        return _ffn(h, g2, w2i, w2o, gf, final_norm=True)

    mp = bsz * seq
    hp, (qa, ka, kab, va, vat, qb, kb, kbb, vb, vbt, qi, ki, kid, wit) = front(
        x_prompt.reshape(mp, d), 0, seq, True)
    r3 = lambda a: a.reshape(bsz, seq, a.shape[-1])
    oa = _diff_prompt(r3(qa), r3(kab), vat, lam_params, gh.T, lam_init=lam_init)
    ob = _dsa_prompt(r3(qb), r3(qi), wit, r3(kid), r3(kbb), vbt)
    y_prompt = back(hp, oa.reshape(mp, -1), ob.reshape(mp, -1)).reshape(bsz, seq, d)

    hs, (qa_s, ka_s, va_s, qb_s, kb_s, vb_s, qi_s, ki_s, wi_s) = front(
        x_sample.reshape(db, d), past, 1, False)
    oa_s = _diff_decode(
        page_table, qa_s.reshape(db, H_A, DV_A), ka_s.reshape(db, H_A, DV_A),
        va_s.reshape(db, H_A, DV_A), lam_params, gh,
        cache_diff_k.reshape(n_phys, PAGE_SIZE * H_A, DV_A),
        cache_diff_v.reshape(n_phys, PAGE_SIZE * H_A, DV_A), lam_init=lam_init)
    scores3 = _idx_decode(page_table, qi_s.reshape(db, H_IDX, D_IDX), wi_s.reshape(db, H_IDX, 1),
                          ki_s.reshape(db, 1, D_IDX),
                          jnp.swapaxes(cache_idx_k.reshape(n_phys, PAGE_SIZE, D_IDX), 1, 2))
    thr, jb = _select_decode(scores3.reshape(db, -1), min(TOPK, (past + 1) // 4))
    ob_s = _dsa_decode(
        page_table, thr.reshape(db), jb.reshape(db), qb_s.reshape(db, H_B, DH_B),
        kb_s.reshape(db, KV_B, DH_B), vb_s.reshape(db, KV_B, DH_B), scores3,
        cache_dsa_k.reshape(n_phys, PAGE_SIZE * KV_B, DH_B),
        cache_dsa_v.reshape(n_phys, PAGE_SIZE * KV_B, DH_B))
    y_sample = back(hs, oa_s.reshape(db, -1), ob_s.reshape(db, -1)).reshape(db, 1, d)

    p5 = lambda a, h, w: a.reshape(1, bsz, seq, h, w)
    s5 = lambda a, h, w: a.reshape(1, db, 1, h, w)
    return (y_prompt, y_sample,
            p5(ka, H_A, 2 * D_A), p5(va, H_A, DV_A), p5(kb, KV_B, DH_B), p5(vb, KV_B, DH_B),
            ki.reshape(1, bsz, seq, D_IDX),
            s5(ka_s, H_A, 2 * D_A), s5(va_s, H_A, DV_A), s5(kb_s, KV_B, DH_B), s5(vb_s, KV_B, DH_B),
            ki_s.reshape(1, db, 1, D_IDX))
```

```python
import functools
import math

import jax
import jax.numpy as jnp
from jax import lax
from jax.experimental import pallas as pl
from jax.experimental.pallas import tpu as pltpu

H_A = 8
D_A = 64
DV_A = 2 * D_A
H_B = 8
KV_B = 2
DH_B = 128
H_IDX = 16
D_IDX = 64
TOPK = 256
ROPE_THETA = 10000.0
EPS = 1e-6
PAGE_SIZE = 128

LANES = 128
KV_TILE = 256
LOG2E = 1.4426950408889634
IDX_PAGES_PER_STEP = 32
DSA_PAGES_PER_STEP = 16
NEG_BIG = -1e30
INT_MIN = -(2 ** 31)
VMEM_LIMIT = 56 * 1024 * 1024

W_QA = H_A * 2 * D_A
W_KA = H_A * 2 * D_A
W_VA = H_A * DV_A
W_QB = H_B * DH_B
W_KB = KV_B * DH_B
W_VB = KV_B * DH_B
W_QI = H_IDX * D_IDX
W_MAIN = W_QA + W_KA + W_VA + W_QB + W_KB + W_VB + W_QI

F32 = jnp.float32
BF16 = jnp.bfloat16
NT_DIMS = (((1,), (1,)), ((), ()))


def _cparams(sem):
    return pltpu.CompilerParams(dimension_semantics=sem, vmem_limit_bytes=VMEM_LIMIT)


def _rms(x, g):
    return x * lax.rsqrt(jnp.mean(x * x, axis=-1, keepdims=True) + EPS) * g


def _sort_key(x):
    b = lax.bitcast_convert_type(x, jnp.int32)
    return b ^ ((b >> 31) & jnp.int32(0x7FFFFFFF))


def _kth_largest_key(count_ge, like, k):
    def body(it, t):
        c = t | lax.shift_left(jnp.int32(1), 31 - it)
        cnt = count_ge(c ^ jnp.int32(INT_MIN))
        return jnp.where(cnt >= k, c, t)

    t = lax.fori_loop(0, 32, body, jnp.zeros_like(like))
    return t ^ jnp.int32(INT_MIN)


def _tie_index_bound(count_eq_below, need, like, nbits):
    def body(it, jb):
        c = jb | lax.shift_left(jnp.int32(1), nbits - 1 - it)
        return jnp.where(count_eq_below(c) <= need, c, jb)

    return lax.fori_loop(0, nbits, body, jnp.zeros_like(like))


def _ffn_body(x_ref, g_ref, wg_ref, wu_ref, wo_ref, gf_ref, o_ref, xn_ref, acc_ref, *, final_norm):
    j = pl.program_id(1)

    @pl.when(j == 0)
    def _():
        xn_ref[...] = _rms(x_ref[...], g_ref[...]).astype(BF16)
        acc_ref[...] = jnp.zeros_like(acc_ref)

    xn = xn_ref[...]
    gate = jnp.dot(xn, wg_ref[...], preferred_element_type=F32)
    up = jnp.dot(xn, wu_ref[...], preferred_element_type=F32)
    act = (gate * jax.nn.sigmoid(gate) * up).astype(BF16)
    acc_ref[...] += jnp.dot(act, wo_ref[...], preferred_element_type=F32)

    @pl.when(j == pl.num_programs(1) - 1)
    def _():
        h = x_ref[...] + 0.5 * acc_ref[...]
        if final_norm:
            h = _rms(h, gf_ref[...])
        o_ref[...] = h


def _ffn(x, g, w_in, w_out, g_final, *, final_norm):
    m, d = x.shape
    f = w_out.shape[0]
    tm = min(m, 512)
    tf = min(f, 512)
    assert m % tm == 0 and f % tf == 0
    nf = f // tf
    return pl.pallas_call(
        functools.partial(_ffn_body, final_norm=final_norm),
        grid=(m // tm, nf),
        in_specs=[
            pl.BlockSpec((tm, d), lambda i, j: (i, 0)),
            pl.BlockSpec((1, d), lambda i, j: (0, 0)),
            pl.BlockSpec((d, tf), lambda i, j: (0, j)),
            pl.BlockSpec((d, tf), lambda i, j: (0, j + nf)),
            pl.BlockSpec((tf, d), lambda i, j: (j, 0)),
            pl.BlockSpec((1, d), lambda i, j: (0, 0)),
        ],
        out_specs=pl.BlockSpec((tm, d), lambda i, j: (i, 0)),
        out_shape=jax.ShapeDtypeStruct((m, d), F32),
        scratch_shapes=[pltpu.VMEM((tm, d), BF16), pltpu.VMEM((tm, d), F32)],
        compiler_params=_cparams(("parallel", "arbitrary")),
        name="ffn_final" if final_norm else "ffn",
    )(x, g, w_in, w_in, w_out, g_final)


def _inproj_body(x_ref, g_ref, wm_ref, wk_ref, ww_ref, inv64_ref, inv128_ref, *out_refs,
                 tm, pos_offset, period, prompt):
    if prompt:
        (qa_ref, ka_ref, kab_ref, va_ref, vat_ref, qb_ref, kb_ref, kbb_ref, vb_ref, vbt_ref,
         qi_ref, ki_ref, kid_ref, wi_ref) = out_refs
    else:
        qa_ref, ka_ref, va_ref, qb_ref, kb_ref, vb_ref, qi_ref, ki_ref, wi_ref = out_refs
    xn = _rms(x_ref[...], g_ref[...]).astype(BF16)
    if period == 1:
        pos = jnp.full((tm, 1), float(pos_offset), F32)
    else:
        row0 = (pl.program_id(0) * tm) % period
        pos = (pos_offset + row0 + lax.broadcasted_iota(jnp.int32, (tm, 1), 0)).astype(F32)
    lane = lax.broadcasted_iota(jnp.int32, (1, LANES), 1)
    lo64 = (lane % 64) < 32
    lo128 = lane < 64
    ang64 = pos * inv64_ref[...]
    ang128 = pos * inv128_ref[...]
    cos64 = jnp.cos(ang64)
    sin64 = jnp.where(lo64, -jnp.sin(ang64), jnp.sin(ang64))
    cos128 = jnp.cos(ang128)
    sin128 = jnp.where(lo128, -jnp.sin(ang128), jnp.sin(ang128))

    def rope64(z):
        rot = jnp.where(lo64, pltpu.roll(z, 96, 1), pltpu.roll(z, 32, 1))
        return z * cos64 + rot * sin64

    def rope128(z):
        return z * cos128 + pltpu.roll(z, 64, 1) * sin128

    def rows(ref):
        def put(col, zc):
            ref[:, col:col + LANES] = zc.astype(ref.dtype)
        return put

    def tiles(ref):
        def put(col, zc):
            ref[col // LANES] = zc.T.astype(ref.dtype)
        return put

    def region(start, width, fn, sinks):
        step = 512 if width % 512 == 0 else width
        for c0 in range(0, width, step):
            z = jnp.dot(xn, wm_ref[:, start + c0:start + c0 + step], preferred_element_type=F32)
            for c in range(0, step, LANES):
                zc = fn(z[:, c:c + LANES])
                for put in sinks:
                    put(c0 + c, zc)

    ident = lambda z: z
    off = 0
    region(off, W_QA, lambda z: rope64(z) * (D_A ** -0.5), [rows(qa_ref)]); off += W_QA
    region(off, W_KA, rope64, [rows(ka_ref)] + ([rows(kab_ref)] if prompt else [])); off += W_KA
    region(off, W_VA, ident, [rows(va_ref)] + ([tiles(vat_ref)] if prompt else [])); off += W_VA
    region(off, W_QB, rope128, [rows(qb_ref)]); off += W_QB
    region(off, W_KB, rope128, [rows(kb_ref)] + ([rows(kbb_ref)] if prompt else [])); off += W_KB
    region(off, W_VB, ident, [rows(vb_ref)] + ([tiles(vbt_ref)] if prompt else [])); off += W_VB
    region(off, W_QI, lambda z: rope64(z) * (D_IDX ** -0.5), [rows(qi_ref)]); off += W_QI
    kd = rope64(jnp.dot(xn, wk_ref[...], preferred_element_type=F32))
    ki_ref[...] = kd[:, :D_IDX]
    wz = jnp.dot(xn, ww_ref[...], preferred_element_type=F32) * (H_IDX ** -0.5)
    if prompt:
        kid_ref[...] = kd.astype(BF16)
        wi_ref[...] = wz.T[:H_IDX]
    else:
        wi_ref[...] = wz[:, :H_IDX]


def _inproj(x, g, w_main, w_kidx2, w_widx, inv64, inv128, *, pos_offset, period, prompt):
    m, d = x.shape
    tm = min(m, KV_TILE)
    assert m % tm == 0 and (period == 1 or period % tm == 0)
    nt = m // tm
    row = lambda w, dt: (jax.ShapeDtypeStruct((m, w), dt), pl.BlockSpec((tm, w), lambda i: (i, 0)))
    tile = lambda h: (jax.ShapeDtypeStruct((h, LANES, m), BF16),
                      pl.BlockSpec((h, LANES, tm), lambda i: (0, 0, i)))
    const = lambda a: pl.BlockSpec(a.shape, lambda i: (0, 0), pipeline_mode=pl.Buffered(1))
    if prompt:
        outs = [row(W_QA, BF16), row(W_KA, F32), row(W_KA, BF16), row(W_VA, F32), tile(H_A),
                row(W_QB, BF16), row(W_KB, F32), row(W_KB, BF16), row(W_VB, F32), tile(KV_B),
                row(W_QI, BF16), row(D_IDX, F32), row(2 * D_IDX, BF16),
                (jax.ShapeDtypeStruct((H_IDX, m), F32), pl.BlockSpec((H_IDX, tm), lambda i: (0, i)))]
    else:
        outs = [row(W_QA, BF16), row(W_KA, F32), row(W_VA, F32), row(W_QB, BF16), row(W_KB, F32),
                row(W_VB, F32), row(W_QI, BF16), row(D_IDX, F32), row(H_IDX, F32)]
    return pl.pallas_call(
        functools.partial(_inproj_body, tm=tm, pos_offset=pos_offset, period=period, prompt=prompt),
        grid=(nt,),
        in_specs=[pl.BlockSpec((tm, d), lambda i: (i, 0)), const(g), const(w_main), const(w_kidx2),
                  const(w_widx), const(inv64), const(inv128)],
        out_specs=[o[1] for o in outs],
        out_shape=[o[0] for o in outs],
        compiler_params=_cparams(("parallel",)),
        name="inproj",
    )(x, g, w_main, w_kidx2, w_widx, inv64, inv128)


def _lambda_value(lam_ref, lam_init):
    lp = lam_ref[...]
    a = jnp.sum(lp[0:1] * lp[1:2], axis=-1, keepdims=True)
    b = jnp.sum(lp[2:3] * lp[3:4], axis=-1, keepdims=True)
    return jnp.exp(a) - jnp.exp(b) + lam_init


def _diff_prompt_body(q_ref, k_ref, vt_ref, lam_ref, ght_ref, o_ref, s_ref, p_ref, *, tq, nt, lam_init):
    i = pl.program_id(2)
    lane = lax.broadcasted_iota(jnp.int32, (1, LANES), 1)
    q = q_ref[0]
    zero = jnp.zeros_like(q)
    qq = jnp.concatenate([jnp.where(lane < D_A, q, zero), jnp.where(lane >= D_A, q, zero)], axis=0)
    nq = 2 * tq
    kr = lax.broadcasted_iota(jnp.int32, (tq, tq), 0)
    qc = lax.broadcasted_iota(jnp.int32, (tq, tq), 1)
    bias = jnp.where(kr <= qc, 0.0, NEG_BIG).astype(F32)
    bias = jnp.concatenate([bias, bias], axis=1)
    lam = _lambda_value(lam_ref, lam_init)

    def attend(n):
        keys = n * tq
        s_ref[0:keys, :] = lax.dot_general(k_ref[0, 0:keys, :], qq, NT_DIMS, preferred_element_type=F32)
        mx = jnp.full((1, nq), NEG_BIG, F32)
        for j in range(n):
            blk = s_ref[j * tq:(j + 1) * tq, :]
            if j == n - 1:
                blk = blk + bias
                s_ref[j * tq:(j + 1) * tq, :] = blk
            mx = jnp.maximum(mx, jnp.max(blk, axis=0, keepdims=True))
        l = jnp.zeros((1, nq), F32)
        for j in range(n):
            p = jnp.exp(s_ref[j * tq:(j + 1) * tq, :] - mx)
            p_ref[j * tq:(j + 1) * tq, :] = p.astype(BF16)
            l = l + jnp.sum(p, axis=0, keepdims=True)
        acc = jnp.dot(vt_ref[0, :, 0:keys], p_ref[0:keys, :], preferred_element_type=F32)
        o = acc * (1.0 / l)
        o = o[:, :tq] - lam * o[:, tq:]
        o = o * lax.rsqrt(jnp.mean(o * o, axis=0, keepdims=True) + EPS) * (ght_ref[...] * (1.0 - lam_init))
        o_ref[0] = o.T.astype(o_ref.dtype)

    for n in range(1, nt + 1):
        pl.when(i == n - 1)(functools.partial(attend, n))


def _diff_prompt(qa, ka, va_t, lam_params, g_head_t, *, lam_init):
    b, s, _ = qa.shape
    tq = KV_TILE
    nt = s // tq
    assert s % tq == 0 and va_t.shape == (H_A, DV_A, b * s)
    return pl.pallas_call(
        functools.partial(_diff_prompt_body, tq=tq, nt=nt, lam_init=lam_init),
        grid=(b, H_A, nt),
        in_specs=[
            pl.BlockSpec((1, tq, DV_A), lambda bb, h, i: (bb, i, h)),
            pl.BlockSpec((1, s, DV_A), lambda bb, h, i: (bb, 0, h)),
            pl.BlockSpec((1, DV_A, s), lambda bb, h, i: (h, 0, bb)),
            pl.BlockSpec(lam_params.shape, lambda bb, h, i: (0, 0)),
            pl.BlockSpec(g_head_t.shape, lambda bb, h, i: (0, 0)),
        ],
        out_specs=pl.BlockSpec((1, tq, DV_A), lambda bb, h, i: (bb, i, h)),
        out_shape=jax.ShapeDtypeStruct((b, s, H_A * DV_A), BF16),
        scratch_shapes=[pltpu.VMEM((s, 2 * tq), F32), pltpu.VMEM((s, 2 * tq), BF16)],
        compiler_params=_cparams(("parallel", "parallel", "parallel")),
        name="diff_prompt",
    )(qa, ka, va_t, lam_params, g_head_t)


def _dsa_prompt_body(qb_ref, qi_ref, wit_ref, kd_ref, kb_ref, vbt_ref, o_ref,
                     key_ref, qs_ref, qg_ref, bias_ref, s_ref, p_ref, *, tq, tk, nt, topk, idx_bits):
    i = pl.program_id(1)
    nkv = (i * tq + tq - 1) // tk + 1
    lane = lax.broadcasted_iota(jnp.int32, (1, LANES), 1)
    krow = lax.broadcasted_iota(jnp.int32, (tk, tq), 0)
    qcol = lax.broadcasted_iota(jnp.int32, (tk, tq), 1) + i * tq

    for h in range(H_IDX):
        blk = qi_ref[0, :, (h // 2) * LANES:(h // 2 + 1) * LANES]
        keep = (lane >= D_IDX) if h % 2 else (lane < D_IDX)
        qs_ref[h * tq:(h + 1) * tq, :] = jnp.where(keep, blk, jnp.zeros_like(blk))
    wt = wit_ref[...]

    def score_block(j, _):
        start = pl.multiple_of(j * tk, tk)
        r = lax.dot_general(kd_ref[0, pl.ds(start, tk), :], qs_ref[...], NT_DIMS,
                            preferred_element_type=F32)
        acc = jnp.zeros((tk, tq), F32)
        for h in range(H_IDX):
            acc = acc + wt[h:h + 1, :] * jnp.maximum(r[:, h * tq:(h + 1) * tq], 0.0)
        acc = jnp.where(krow + j * tk <= qcol, acc, -jnp.inf)
        key_ref[j] = _sort_key(acc)
        return 0

    lax.fori_loop(0, nkv, score_block, 0)

    def count(pred_fn):
        def blk(j, part):
            hit = jnp.where(pred_fn(key_ref[j], j), 1.0, 0.0)
            parts = [hit[r:r + 8] for r in range(0, tk, 8)]
            while len(parts) > 1:
                parts = [parts[a] + parts[a + 1] for a in range(0, len(parts), 2)]
            return part + parts[0]
        part = lax.fori_loop(0, nkv, blk, jnp.zeros((8, tq), F32))
        return jnp.sum(part, axis=0, keepdims=True).astype(jnp.int32)

    like = jnp.zeros((1, tq), jnp.int32)
    thr = _kth_largest_key(lambda c: count(lambda kk, j: kk >= c), like, topk)
    need = topk - count(lambda kk, j: kk > thr)
    n_eq = count(lambda kk, j: kk == thr)
    has_ties = jnp.max(jnp.where(n_eq != need, 1.0, 0.0)) > 0.5
    jbound = lax.cond(
        has_ties,
        lambda: _tie_index_bound(
            lambda c: count(lambda kk, j: (kk == thr) & (krow + j * tk < c)), need, like, idx_bits),
        lambda: jnp.full((1, tq), 2 ** idx_bits, jnp.int32))

    scale2 = (DH_B ** -0.5) * LOG2E
    group = H_B // KV_B
    for g in range(KV_B):
        for hh in range(group):
            qg_ref[g, hh * tq:(hh + 1) * tq, :] = qb_ref[0, :, (g * group + hh) * DH_B:(g * group + hh + 1) * DH_B]

    def attend(n):
        keys = n * tk
        for j in range(n):
            kk = key_ref[j]
            kidx = krow + j * tk
            sel = ((kk > thr) | ((kk == thr) & (kidx < jbound))) & (kidx <= qcol)
            bias_ref[j * tk:(j + 1) * tk, :] = jnp.where(sel, 0.0, NEG_BIG).astype(F32)
        nq = group * tq
        for g in range(KV_B):
            s_ref[0:keys, :] = lax.dot_general(kb_ref[0, 0:keys, g * DH_B:(g + 1) * DH_B], qg_ref[g], NT_DIMS,
                                               preferred_element_type=F32)
            mx = jnp.full((1, nq), NEG_BIG, F32)
            for j in range(n):
                b1 = bias_ref[j * tk:(j + 1) * tk, :]
                blk = s_ref[j * tk:(j + 1) * tk, :] * scale2 + jnp.concatenate([b1] * group, axis=1)
                s_ref[j * tk:(j + 1) * tk, :] = blk
                mx = jnp.maximum(mx, jnp.max(blk, axis=0, keepdims=True))
            l = jnp.zeros((1, nq), F32)
            for j in range(n):
                p = jnp.exp2(s_ref[j * tk:(j + 1) * tk, :] - mx)
                p_ref[j * tk:(j + 1) * tk, :] = p.astype(BF16)
                l = l + jnp.sum(p, axis=0, keepdims=True)
            o = jnp.dot(vbt_ref[g, :, 0:keys], p_ref[0:keys, :], preferred_element_type=F32) * (1.0 / l)
            for hh in range(group):
                h = g * group + hh
                o_ref[0, :, h * DH_B:(h + 1) * DH_B] = o[:, hh * tq:(hh + 1) * tq].T.astype(o_ref.dtype)

    for n in range(1, nt + 1):
        pl.when(nkv == n)(functools.partial(attend, n))


def _dsa_prompt(qb, qi, wi_t, kid, kb, vb_t):
    b, s, _ = qb.shape
    tq = min(s, LANES)
    tk = KV_TILE
    nt = s // tk
    topk = min(TOPK, s // 4)
    assert s % tq == 0 and s % tk == 0 and tk >= topk and tk % tq == 0
    assert vb_t.shape == (KV_B, DH_B, b * s) and wi_t.shape == (H_IDX, b * s)
    idx_bits = int(math.ceil(math.log2(s))) + 1
    blk_q = lambda w: pl.BlockSpec((1, tq, w), lambda bb, i: (bb, i, 0))
    blk_s = lambda w: pl.BlockSpec((1, s, w), lambda bb, i: (bb, 0, 0))
    return pl.pallas_call(
        functools.partial(_dsa_prompt_body, tq=tq, tk=tk, nt=nt, topk=topk, idx_bits=idx_bits),
        grid=(b, s // tq),
        in_specs=[blk_q(W_QB), blk_q(W_QI),
                  pl.BlockSpec((H_IDX, tq), lambda bb, i: (0, bb * (s // tq) + i)),
                  blk_s(2 * D_IDX), blk_s(W_KB),
                  pl.BlockSpec((KV_B, DH_B, s), lambda bb, i: (0, 0, bb))],
        out_specs=blk_q(W_QB),
        out_shape=jax.ShapeDtypeStruct((b, s, W_QB), BF16),
        scratch_shapes=[
            pltpu.VMEM((nt, tk, tq), jnp.int32),
            pltpu.VMEM((H_IDX * tq, LANES), BF16),
            pltpu.VMEM((KV_B, (H_B // KV_B) * tq, DH_B), BF16),
            pltpu.VMEM((s, tq), F32),
            pltpu.VMEM((s, (H_B // KV_B) * tq), F32),
            pltpu.VMEM((s, (H_B // KV_B) * tq), BF16),
        ],
        compiler_params=_cparams(("parallel", "parallel")),
        name="dsa_prompt",
    )(qb, qi, wi_t, kid, kb, vb_t)


def _outproj_body(h_ref, oa_ref, ob_ref, wa_ref, wb_ref, o_ref):
    o_ref[...] = (h_ref[...]
                  + jnp.dot(oa_ref[...], wa_ref[...], preferred_element_type=F32)
                  + jnp.dot(ob_ref[...], wb_ref[...], preferred_element_type=F32))


def _outproj(h, oa, ob, w_a, w_b):
    m, d = h.shape
    tm = min(m, 512)
    assert m % tm == 0
    row = lambda w: pl.BlockSpec((tm, w), lambda i: (i, 0))
    const = lambda a: pl.BlockSpec(a.shape, lambda i: (0, 0), pipeline_mode=pl.Buffered(1))
    return pl.pallas_call(
        _outproj_body,
        grid=(m // tm,),
        in_specs=[row(d), row(oa.shape[1]), row(ob.shape[1]), const(w_a), const(w_b)],
        out_specs=row(d),
        out_shape=jax.ShapeDtypeStruct((m, d), F32),
        compiler_params=_cparams(("parallel",)),
        name="outproj",
    )(h, oa, ob, w_a, w_b)


def _diff_decode_body(pt_ref, q_ref, ks_ref, vs_ref, lam_ref, gh_ref, *rest, pp, lam_init):
    k_refs = rest[:pp]
    v_refs = rest[pp:2 * pp]
    o_ref, m_ref, l_ref, acc_ref = rest[2 * pp:]
    j = pl.program_id(1)
    q = q_ref[0]
    lane = lax.broadcasted_iota(jnp.int32, (1, LANES), 1)
    zero = jnp.zeros_like(q)
    qq = jnp.concatenate([jnp.where(lane < D_A, q, zero), jnp.where(lane >= D_A, q, zero)], axis=0)
    rows_per_page = PAGE_SIZE * H_A

    @pl.when(j == 0)
    def _():
        ks = ks_ref[0].astype(BF16).astype(F32)
        prod = q.astype(F32) * ks
        s1 = jnp.sum(jnp.where(lane < D_A, prod, 0.0), axis=-1, keepdims=True)
        s2 = jnp.sum(jnp.where(lane >= D_A, prod, 0.0), axis=-1, keepdims=True)
        m_ref[...] = jnp.concatenate([s1, s2], axis=0)
        l_ref[...] = jnp.ones_like(l_ref)
        vs = vs_ref[0].astype(BF16).astype(F32)
        acc_ref[...] = jnp.concatenate([vs, vs], axis=0)

    head_of_row = lax.broadcasted_iota(jnp.int32, (2 * H_A, rows_per_page), 0) % H_A
    head_of_lane = lax.broadcasted_iota(jnp.int32, (2 * H_A, rows_per_page), 1) % H_A
    bias = jnp.where(head_of_row == head_of_lane, 0.0, NEG_BIG).astype(F32)
    s_all = []
    for p in range(pp):
        kp = k_refs[p][0].astype(BF16)
        s_all.append(lax.dot_general(qq, kp, NT_DIMS, preferred_element_type=F32) + bias)
    m = m_ref[...]
    m_new = m
    for s in s_all:
        m_new = jnp.maximum(m_new, jnp.max(s, axis=-1, keepdims=True))
    alpha = jnp.exp(m - m_new)
    l = alpha * l_ref[...]
    acc = alpha * acc_ref[...]
    for p in range(pp):
        pr = jnp.exp(s_all[p] - m_new)
        l = l + jnp.sum(pr, axis=-1, keepdims=True)
        acc = acc + jnp.dot(pr.astype(BF16), v_refs[p][0].astype(BF16), preferred_element_type=F32)
    m_ref[...] = m_new
    l_ref[...] = l
    acc_ref[...] = acc

    @pl.when(j == pl.num_programs(1) - 1)
    def _():
        o = acc / l
        lam = _lambda_value(lam_ref, lam_init)
        o = o[:H_A] - lam * o[H_A:]
        o_ref[0] = (_rms(o, gh_ref[...]) * (1.0 - lam_init)).astype(o_ref.dtype)


def _diff_decode(page_table, qa, k_self, v_self, lam_params, g_head, cache_k, cache_v, *, lam_init):
    db, n_pages = page_table.shape
    pp = 8 if n_pages % 8 == 0 else 1
    rows = PAGE_SIZE * H_A
    tok = pl.BlockSpec((1, H_A, DV_A), lambda b, j, pt: (b, 0, 0))
    page = lambda p: pl.BlockSpec((1, rows, DV_A), lambda b, j, pt: (pt[b, j * pp + p], 0, 0))
    grid_spec = pltpu.PrefetchScalarGridSpec(
        num_scalar_prefetch=1,
        grid=(db, n_pages // pp),
        in_specs=[tok, tok, tok,
                  pl.BlockSpec(lam_params.shape, lambda b, j, pt: (0, 0)),
                  pl.BlockSpec(g_head.shape, lambda b, j, pt: (0, 0))]
                 + [page(p) for p in range(pp)] + [page(p) for p in range(pp)],
        out_specs=tok,
        scratch_shapes=[pltpu.VMEM((2 * H_A, 1), F32), pltpu.VMEM((2 * H_A, 1), F32),
                        pltpu.VMEM((2 * H_A, DV_A), F32)],
    )
    return pl.pallas_call(
        functools.partial(_diff_decode_body, pp=pp, lam_init=lam_init),
        grid_spec=grid_spec,
        out_shape=jax.ShapeDtypeStruct((db, H_A, DV_A), BF16),
        compiler_params=_cparams(("parallel", "arbitrary")),
        name="diff_decode",
    )(page_table, qa, k_self, v_self, lam_params, g_head, *([cache_k] * pp), *([cache_v] * pp))


def _idx_decode_body(pt_ref, qs_ref, w_ref, kself_ref, *rest, pp, n_steps):
    k_refs = rest[:pp]
    o_ref = rest[pp]
    j = pl.program_id(1)
    qs = qs_ref[0]
    w = w_ref[0]

    @pl.when(j < n_steps)
    def _():
        for p in range(pp):
            kp = k_refs[p][0].astype(BF16)
            r = jnp.dot(qs, kp, preferred_element_type=F32)
            o_ref[0, p:p + 1, :] = jnp.sum(w * jnp.maximum(r, 0.0), axis=0, keepdims=True)

    @pl.when(j == n_steps)
    def _():
        ks = kself_ref[0].astype(BF16).astype(F32)
        r = jnp.sum(qs.astype(F32) * ks, axis=-1, keepdims=True)
        sc = jnp.sum(w * jnp.maximum(r, 0.0), axis=0, keepdims=True)
        lane = lax.broadcasted_iota(jnp.int32, (1, LANES), 1)
        o_ref[0] = jnp.full((pp, LANES), -jnp.inf, F32)
        o_ref[0, 0:1, :] = jnp.where(lane == 0, sc, -jnp.inf)


def _idx_decode(page_table, qs, w, k_self, cache_idx_t):
    db, n_pages = page_table.shape
    pp = min(IDX_PAGES_PER_STEP, n_pages)
    assert n_pages % pp == 0
    n_steps = n_pages // pp
    page = lambda p: pl.BlockSpec(
        (1, D_IDX, PAGE_SIZE),
        lambda b, j, pt: (pt[b, jnp.minimum(j * pp + p, n_pages - 1)], 0, 0))
    grid_spec = pltpu.PrefetchScalarGridSpec(
        num_scalar_prefetch=1,
        grid=(db, n_steps + 1),
        in_specs=[pl.BlockSpec((1, H_IDX, D_IDX), lambda b, j, pt: (b, 0, 0)),
                  pl.BlockSpec((1, H_IDX, 1), lambda b, j, pt: (b, 0, 0)),
                  pl.BlockSpec((1, 1, D_IDX), lambda b, j, pt: (b, 0, 0))]
                 + [page(p) for p in range(pp)],
        out_specs=pl.BlockSpec((1, pp, LANES), lambda b, j, pt: (b, j, 0)),
    )
    return pl.pallas_call(
        functools.partial(_idx_decode_body, pp=pp, n_steps=n_steps),
        grid_spec=grid_spec,
        out_shape=jax.ShapeDtypeStruct((db, n_pages + pp, LANES), F32),
        compiler_params=_cparams(("parallel", "arbitrary")),
        name="idx_decode",
    )(page_table, qs, w, k_self, *([cache_idx_t] * pp))


def _select_decode_body(s_ref, thr_ref, jb_ref, *, topk, idx_bits):
    keys = _sort_key(s_ref[...])
    n = keys.shape[1]
    col = lax.broadcasted_iota(jnp.int32, keys.shape, 1)

    def row_count(pred):
        hit = jnp.where(pred, 1.0, 0.0)
        part = jnp.zeros((keys.shape[0], LANES), F32)
        for c in range(0, n, LANES):
            part = part + hit[:, c:c + LANES]
        return jnp.sum(part, axis=-1, keepdims=True).astype(jnp.int32)

    like = jnp.zeros((keys.shape[0], 1), jnp.int32)
    thr = _kth_largest_key(lambda c: row_count(keys >= c), like, topk)
    need = topk - row_count(keys > thr)
    jb = _tie_index_bound(lambda c: row_count((keys == thr) & (col < c)), need, like, idx_bits)
    thr_ref[...] = thr
    jb_ref[...] = jb


def _select_decode(scores, topk):
    db, n = scores.shape
    idx_bits = int(math.ceil(math.log2(n))) + 1
    return pl.pallas_call(
        functools.partial(_select_decode_body, topk=topk, idx_bits=idx_bits),
        out_shape=[jax.ShapeDtypeStruct((db, 1), jnp.int32)] * 2,
        compiler_params=pltpu.CompilerParams(vmem_limit_bytes=VMEM_LIMIT),
        name="select_decode",
    )(scores)


def _dsa_decode_body(pt_ref, thr_ref, jb_ref, q_ref, ks_ref, vs_ref, sc_ref, *rest, pp, n_steps):
    k_refs = rest[:pp]
    v_refs = rest[pp:2 * pp]
    o_ref, m_ref, l_ref, acc_ref = rest[2 * pp:]
    b = pl.program_id(0)
    j = pl.program_id(1)
    q = q_ref[0]
    thr = thr_ref[b]
    jb = jb_ref[b]
    scale = DH_B ** -0.5
    group = H_B // KV_B
    rows_per_page = PAGE_SIZE * KV_B

    @pl.when(j == 0)
    def _():
        m_ref[...] = jnp.full_like(m_ref, NEG_BIG)
        l_ref[...] = jnp.zeros_like(l_ref)
        acc_ref[...] = jnp.zeros_like(acc_ref)

    keys = _sort_key(sc_ref[0])
    idx = (lax.broadcasted_iota(jnp.int32, (pp, LANES), 0) + j * pp) * PAGE_SIZE \
        + lax.broadcasted_iota(jnp.int32, (pp, LANES), 1)
    sel = (keys > thr) | ((keys == thr) & (idx < jb))

    @pl.when(j < n_steps)
    def _():
        spread = (lax.broadcasted_iota(jnp.int32, (LANES, rows_per_page), 1) // KV_B
                  == lax.broadcasted_iota(jnp.int32, (LANES, rows_per_page), 0))
        sel2 = jnp.dot(jnp.where(sel, 1.0, 0.0).astype(BF16), jnp.where(spread, 1.0, 0.0).astype(BF16),
                       preferred_element_type=F32) > 0.5
        kv_of_row = lax.broadcasted_iota(jnp.int32, (H_B, rows_per_page), 0) // group
        kv_of_lane = lax.broadcasted_iota(jnp.int32, (H_B, rows_per_page), 1) % KV_B
        same_kv = kv_of_row == kv_of_lane
        s_all, ok_all = [], []
        for p in range(pp):
            kp = k_refs[p][0].astype(BF16)
            s = lax.dot_general(q, kp, NT_DIMS, preferred_element_type=F32) * scale
            ok = same_kv & sel2[p:p + 1, :]
            s_all.append(jnp.where(ok, s, NEG_BIG))
            ok_all.append(ok)
        m = m_ref[...]
        m_new = m
        for s in s_all:
            m_new = jnp.maximum(m_new, jnp.max(s, axis=-1, keepdims=True))
        alpha = jnp.exp(m - m_new)
        l = alpha * l_ref[...]
        acc = alpha * acc_ref[...]
        for p in range(pp):
            pr = jnp.where(ok_all[p], jnp.exp(s_all[p] - m_new), 0.0)
            l = l + jnp.sum(pr, axis=-1, keepdims=True)
            acc = acc + jnp.dot(pr.astype(BF16), v_refs[p][0].astype(BF16), preferred_element_type=F32)
        m_ref[...] = m_new
        l_ref[...] = l
        acc_ref[...] = acc

    @pl.when(j == n_steps)
    def _():
        row8 = lax.broadcasted_iota(jnp.int32, (H_B, LANES), 0)
        ks = ks_ref[0].astype(BF16).astype(F32)
        vs = vs_ref[0].astype(BF16).astype(F32)
        kexp = jnp.where(row8 < group, ks[0:1], ks[1:2])
        vexp = jnp.where(row8 < group, vs[0:1], vs[1:2])
        s = jnp.sum(q.astype(F32) * kexp, axis=-1, keepdims=True) * scale
        ok = sel[0:1, 0:1]
        s = jnp.where(ok, s, NEG_BIG)
        m = m_ref[...]
        m_new = jnp.maximum(m, s)
        alpha = jnp.exp(m - m_new)
        pr = jnp.where(ok, jnp.exp(s - m_new), 0.0)
        l = alpha * l_ref[...] + pr
        acc = alpha * acc_ref[...] + pr * vexp
        o_ref[0] = (acc / l).astype(o_ref.dtype)


def _dsa_decode(page_table, thr, jb, qb, k_self, v_self, scores3, cache_k, cache_v):
    db, n_pages = page_table.shape
    pp = min(DSA_PAGES_PER_STEP, n_pages)
    assert n_pages % pp == 0 and scores3.shape[1] >= n_pages + pp
    n_steps = n_pages // pp
    rows = PAGE_SIZE * KV_B
    page = lambda p: pl.BlockSpec(
        (1, rows, DH_B),
        lambda b, j, pt, t, jj: (pt[b, jnp.minimum(j * pp + p, n_pages - 1)], 0, 0))
    tok = lambda r: pl.BlockSpec((1, r, DH_B), lambda b, j, pt, t, jj: (b, 0, 0))
    grid_spec = pltpu.PrefetchScalarGridSpec(
        num_scalar_prefetch=3,
        grid=(db, n_steps + 1),
        in_specs=[tok(H_B), tok(KV_B), tok(KV_B),
                  pl.BlockSpec((1, pp, LANES), lambda b, j, pt, t, jj: (b, j, 0))]
                 + [page(p) for p in range(pp)] + [page(p) for p in range(pp)],
        out_specs=tok(H_B),
        scratch_shapes=[pltpu.VMEM((H_B, 1), F32), pltpu.VMEM((H_B, 1), F32),
                        pltpu.VMEM((H_B, DH_B), F32)],
    )
    return pl.pallas_call(
        functools.partial(_dsa_decode_body, pp=pp, n_steps=n_steps),
        grid_spec=grid_spec,
        out_shape=jax.ShapeDtypeStruct((db, H_B, DH_B), BF16),
        compiler_params=_cparams(("parallel", "arbitrary")),
        name="dsa_decode",
    )(page_table, thr, jb, qb, k_self, v_self, scores3, *([cache_k] * pp), *([cache_v] * pp))


def _rope_inv(half):
    return ROPE_THETA ** (-jnp.arange(half, dtype=F32) / half)


def kernel(x_prompt, x_sample, cache_diff_k, cache_diff_v, cache_dsa_k, cache_dsa_v, cache_idx_k,
           page_table, norm_ffn1, w_ffn1_in, w_ffn1_out, norm_mix, w_in, lambda_q1, lambda_k1,
           lambda_q2, lambda_k2, norm_head, w_out, norm_ffn2, w_ffn2_in, w_ffn2_out, norm_final):
    depth = w_in.shape[0]
    assert depth == 1 and x_sample.shape[1] == 1
    bsz, seq, d = x_prompt.shape
    db = x_sample.shape[0]
    n_phys = cache_diff_k.shape[1]
    n_pages = page_table.shape[1]
    past = n_pages * PAGE_SIZE
    lam_init = 0.8 - 0.6 * math.exp(-0.3 * 0)

    w1i, w1o = w_ffn1_in[0].astype(BF16), w_ffn1_out[0].astype(BF16)
    w2i, w2o = w_ffn2_in[0].astype(BF16), w_ffn2_out[0].astype(BF16)
    wi_all = w_in[0].astype(BF16)
    w_main = wi_all[:, :W_MAIN]
    w_k = wi_all[:, W_MAIN:W_MAIN + D_IDX]
    w_kidx2 = jnp.concatenate([w_k, w_k], axis=1)
    w_widx = jnp.pad(wi_all[:, W_MAIN + D_IDX:], ((0, 0), (0, LANES - H_IDX)))
    wo_all = w_out[0].astype(BF16)
    wo_a, wo_b = wo_all[:H_A * DV_A], wo_all[H_A * DV_A:]
    g1, gm, g2 = norm_ffn1[0][None], norm_mix[0][None], norm_ffn2[0][None]
    gh, gf = norm_head[0][None], norm_final[None]
    lam_params = jnp.stack([lambda_q1[0], lambda_k1[0], lambda_q2[0], lambda_k2[0]]).astype(F32)
    inv32, inv64h = _rope_inv(D_A // 2), _rope_inv(DH_B // 2)
    inv64 = jnp.tile(inv32, LANES // (D_A // 2))[None]
    inv128 = jnp.tile(inv64h, LANES // (DH_B // 2))[None]

    def front(x2d, pos_offset, period, prompt):
        h = _ffn(x2d, g1, w1i, w1o, gf, final_norm=False)
        return h, _inproj(h, gm, w_main, w_kidx2, w_widx, inv64, inv128,
                          pos_offset=pos_offset, period=period, prompt=prompt)

    def back(h, oa, ob):
        h = _outproj(h, oa, ob, wo_a, wo_b)
        return _ffn(h, g2, w2i, w2o, gf, final_norm=True)

    mp = bsz * seq
    hp, (qa, ka, kab, va, vat, qb, kb, kbb, vb, vbt, qi, ki, kid, wit) = front(
        x_prompt.reshape(mp, d), 0, seq, True)
    r3 = lambda a: a.reshape(bsz, seq, a.shape[-1])
    oa = _diff_prompt(r3(qa), r3(kab), vat, lam_params, gh.T, lam_init=lam_init)
    ob = _dsa_prompt(r3(qb), r3(qi), wit, r3(kid), r3(kbb), vbt)
    y_prompt = back(hp, oa.reshape(mp, -1), ob.reshape(mp, -1)).reshape(bsz, seq, d)

    hs, (qa_s, ka_s, va_s, qb_s, kb_s, vb_s, qi_s, ki_s, wi_s) = front(
        x_sample.reshape(db, d), past, 1, False)
    oa_s = _diff_decode(
        page_table, qa_s.reshape(db, H_A, DV_A), ka_s.reshape(db, H_A, DV_A),
        va_s.reshape(db, H_A, DV_A), lam_params, gh,
        cache_diff_k.reshape(n_phys, PAGE_SIZE * H_A, DV_A),
        cache_diff_v.reshape(n_phys, PAGE_SIZE * H_A, DV_A), lam_init=lam_init)
    scores3 = _idx_decode(page_table, qi_s.reshape(db, H_IDX, D_IDX), wi_s.reshape(db, H_IDX, 1),
                          ki_s.reshape(db, 1, D_IDX),
                          jnp.swapaxes(cache_idx_k.reshape(n_phys, PAGE_SIZE, D_IDX), 1, 2))
    thr, jb = _select_decode(scores3.reshape(db, -1), min(TOPK, (past + 1) // 4))
    ob_s = _dsa_decode(
        page_table, thr.reshape(db), jb.reshape(db), qb_s.reshape(db, H_B, DH_B),
        kb_s.reshape(db, KV_B, DH_B), vb_s.reshape(db, KV_B, DH_B), scores3,
        cache_dsa_k.reshape(n_phys, PAGE_SIZE * KV_B, DH_B),
        cache_dsa_v.reshape(n_phys, PAGE_SIZE * KV_B, DH_B))
    y_sample = back(hs, oa_s.reshape(db, -1), ob_s.reshape(db, -1)).reshape(db, 1, d)

    p5 = lambda a, h, w: a.reshape(1, bsz, seq, h, w)
    s5 = lambda a, h, w: a.reshape(1, db, 1, h, w)
    return (y_prompt, y_sample,
            p5(ka, H_A, 2 * D_A), p5(va, H_A, DV_A), p5(kb, KV_B, DH_B), p5(vb, KV_B, DH_B),
            ki.reshape(1, bsz, seq, D_IDX),
            s5(ka_s, H_A, 2 * D_A), s5(va_s, H_A, DV_A), s5(kb_s, KV_B, DH_B), s5(vb_s, KV_B, DH_B),
            ki_s.reshape(1, db, 1, D_IDX))
```

```python
import functools
import math

import jax
import jax.numpy as jnp
from jax import lax
from jax.experimental import pallas as pl
from jax.experimental.pallas import tpu as pltpu

H_A = 8
D_A = 64
DV_A = 2 * D_A
H_B = 8
KV_B = 2
DH_B = 128
H_IDX = 16
D_IDX = 64
TOPK = 256
ROPE_THETA = 10000.0
EPS = 1e-6
PAGE_SIZE = 128

LANES = 128
FFN_ROWS = 1024
DIFF_HEADS_PER_STEP = 2
KV_TILE = 256
LOG2E = 1.4426950408889634
IDX_PAGES_PER_STEP = 32
DSA_PAGES_PER_STEP = 16
NEG_BIG = -1e30
INT_MIN = -(2 ** 31)
VMEM_LIMIT = 56 * 1024 * 1024

W_QA = H_A * 2 * D_A
W_KA = H_A * 2 * D_A
W_VA = H_A * DV_A
W_QB = H_B * DH_B
W_KB = KV_B * DH_B
W_VB = KV_B * DH_B
W_QI = H_IDX * D_IDX
W_MAIN = W_QA + W_KA + W_VA + W_QB + W_KB + W_VB + W_QI

F32 = jnp.float32
BF16 = jnp.bfloat16
NT_DIMS = (((1,), (1,)), ((), ()))


def _cparams(sem):
    return pltpu.CompilerParams(dimension_semantics=sem, vmem_limit_bytes=VMEM_LIMIT)


def _rms(x, g):
    return x * lax.rsqrt(jnp.mean(x * x, axis=-1, keepdims=True) + EPS) * g


def _sort_key(x):
    b = lax.bitcast_convert_type(x, jnp.int32)
    return b ^ ((b >> 31) & jnp.int32(0x7FFFFFFF))


def _kth_largest_key(count_ge, like, k):
    def body(it, t):
        c = t | lax.shift_left(jnp.int32(1), 31 - it)
        cnt = count_ge(c ^ jnp.int32(INT_MIN))
        return jnp.where(cnt >= k, c, t)

    t = lax.fori_loop(0, 32, body, jnp.zeros_like(like))
    return t ^ jnp.int32(INT_MIN)


def _tie_index_bound(count_eq_below, need, like, nbits):
    def body(it, jb):
        c = jb | lax.shift_left(jnp.int32(1), nbits - 1 - it)
        return jnp.where(count_eq_below(c) <= need, c, jb)

    return lax.fori_loop(0, nbits, body, jnp.zeros_like(like))


def _ffn_body(x_ref, g_ref, wg_ref, wu_ref, wo_ref, gf_ref, o_ref, xn_ref, *, final_norm):
    j = pl.program_id(1)

    @pl.when(j == 0)
    def _():
        x = x_ref[...]
        xn_ref[...] = _rms(x, g_ref[...]).astype(BF16)
        o_ref[...] = x

    xn = xn_ref[...]
    gate = jnp.dot(xn, wg_ref[...], preferred_element_type=F32)
    up = jnp.dot(xn, wu_ref[...], preferred_element_type=F32)
    act = (gate * jax.nn.sigmoid(gate) * (0.5 * up)).astype(BF16)
    o_ref[...] += jnp.dot(act, wo_ref[...], preferred_element_type=F32)

    if final_norm:
        @pl.when(j == pl.num_programs(1) - 1)
        def _():
            o_ref[...] = _rms(o_ref[...], gf_ref[...])


def _ffn(x, g, w_in, w_out, g_final, *, final_norm):
    m, d = x.shape
    f = w_out.shape[0]
    tm = min(m, FFN_ROWS)
    tf = min(f, 512)
    assert m % tm == 0 and f % tf == 0
    nf = f // tf
    return pl.pallas_call(
        functools.partial(_ffn_body, final_norm=final_norm),
        grid=(m // tm, nf),
        in_specs=[
            pl.BlockSpec((tm, d), lambda i, j: (i, 0)),
            pl.BlockSpec((1, d), lambda i, j: (0, 0)),
            pl.BlockSpec((d, tf), lambda i, j: (0, j)),
            pl.BlockSpec((d, tf), lambda i, j: (0, j + nf)),
            pl.BlockSpec((tf, d), lambda i, j: (j, 0)),
            pl.BlockSpec((1, d), lambda i, j: (0, 0)),
        ],
        out_specs=pl.BlockSpec((tm, d), lambda i, j: (i, 0)),
        out_shape=jax.ShapeDtypeStruct((m, d), F32),
        scratch_shapes=[pltpu.VMEM((tm, d), BF16)],
        compiler_params=_cparams(("parallel", "arbitrary")),
        name="ffn_final" if final_norm else "ffn",
    )(x, g, w_in, w_in, w_out, g_final)


def _inproj_body(x_ref, g_ref, wm_ref, wk_ref, ww_ref, inv64_ref, inv128_ref, *out_refs,
                 tm, pos_offset, period, prompt):
    if prompt:
        (qa_ref, ka_ref, kab_ref, va_ref, vat_ref, qb_ref, kb_ref, kbb_ref, vb_ref, vbt_ref,
         qi_ref, ki_ref, kid_ref, wi_ref) = out_refs
    else:
        qa_ref, ka_ref, va_ref, qb_ref, kb_ref, vb_ref, qi_ref, ki_ref, wi_ref = out_refs
    xn = _rms(x_ref[...], g_ref[...]).astype(BF16)
    if period == 1:
        pos = jnp.full((tm, 1), float(pos_offset), F32)
    else:
        row0 = (pl.program_id(0) * tm) % period
        pos = (pos_offset + row0 + lax.broadcasted_iota(jnp.int32, (tm, 1), 0)).astype(F32)
    lane = lax.broadcasted_iota(jnp.int32, (1, LANES), 1)
    lo64 = (lane % 64) < 32
    lo128 = lane < 64
    ang64 = pos * inv64_ref[...]
    ang128 = pos * inv128_ref[...]
    cos64 = jnp.cos(ang64)
    sin64 = jnp.where(lo64, -jnp.sin(ang64), jnp.sin(ang64))
    cos128 = jnp.cos(ang128)
    sin128 = jnp.where(lo128, -jnp.sin(ang128), jnp.sin(ang128))

    def rope64(z):
        rot = jnp.where(lo64, pltpu.roll(z, 96, 1), pltpu.roll(z, 32, 1))
        return z * cos64 + rot * sin64

    def rope128(z):
        return z * cos128 + pltpu.roll(z, 64, 1) * sin128

    def rows(ref):
        def put(col, zc):
            ref[:, col:col + LANES] = zc.astype(ref.dtype)
        return put

    def tiles(ref):
        def put(col, zc):
            ref[col // LANES] = zc.T.astype(ref.dtype)
        return put

    def region(start, width, fn, sinks):
        step = 512 if width % 512 == 0 else width
        for c0 in range(0, width, step):
            z = jnp.dot(xn, wm_ref[:, start + c0:start + c0 + step], preferred_element_type=F32)
            for c in range(0, step, LANES):
                zc = fn(z[:, c:c + LANES])
                for put in sinks:
                    put(c0 + c, zc)

    ident = lambda z: z
    off = 0
    region(off, W_QA, lambda z: rope64(z) * (D_A ** -0.5), [rows(qa_ref)]); off += W_QA
    region(off, W_KA, rope64, [rows(ka_ref)] + ([rows(kab_ref)] if prompt else [])); off += W_KA
    region(off, W_VA, ident, [rows(va_ref)] + ([tiles(vat_ref)] if prompt else [])); off += W_VA
    region(off, W_QB, rope128, [rows(qb_ref)]); off += W_QB
    region(off, W_KB, rope128, [rows(kb_ref)] + ([rows(kbb_ref)] if prompt else [])); off += W_KB
    region(off, W_VB, ident, [rows(vb_ref)] + ([tiles(vbt_ref)] if prompt else [])); off += W_VB
    region(off, W_QI, lambda z: rope64(z) * (D_IDX ** -0.5), [rows(qi_ref)]); off += W_QI
    kd = rope64(jnp.dot(xn, wk_ref[...], preferred_element_type=F32))
    ki_ref[...] = kd[:, :D_IDX]
    wz = jnp.dot(xn, ww_ref[...], preferred_element_type=F32) * (H_IDX ** -0.5)
    if prompt:
        kid_ref[...] = kd.astype(BF16)
        wi_ref[...] = wz.T[:H_IDX]
    else:
        wi_ref[...] = wz[:, :H_IDX]


def _inproj(x, g, w_main, w_kidx2, w_widx, inv64, inv128, *, pos_offset, period, prompt):
    m, d = x.shape
    tm = min(m, KV_TILE)
    assert m % tm == 0 and (period == 1 or period % tm == 0)
    nt = m // tm
    row = lambda w, dt: (jax.ShapeDtypeStruct((m, w), dt), pl.BlockSpec((tm, w), lambda i: (i, 0)))
    tile = lambda h: (jax.ShapeDtypeStruct((h, LANES, m), BF16),
                      pl.BlockSpec((h, LANES, tm), lambda i: (0, 0, i)))
    const = lambda a: pl.BlockSpec(a.shape, lambda i: (0, 0), pipeline_mode=pl.Buffered(1))
    if prompt:
        outs = [row(W_QA, BF16), row(W_KA, F32), row(W_KA, BF16), row(W_VA, F32), tile(H_A),
                row(W_QB, BF16), row(W_KB, F32), row(W_KB, BF16), row(W_VB, F32), tile(KV_B),
                row(W_QI, BF16), row(D_IDX, F32), row(2 * D_IDX, BF16),
                (jax.ShapeDtypeStruct((H_IDX, m), F32), pl.BlockSpec((H_IDX, tm), lambda i: (0, i)))]
    else:
        outs = [row(W_QA, BF16), row(W_KA, F32), row(W_VA, F32), row(W_QB, BF16), row(W_KB, F32),
                row(W_VB, F32), row(W_QI, BF16), row(D_IDX, F32), row(H_IDX, F32)]
    return pl.pallas_call(
        functools.partial(_inproj_body, tm=tm, pos_offset=pos_offset, period=period, prompt=prompt),
        grid=(nt,),
        in_specs=[pl.BlockSpec((tm, d), lambda i: (i, 0)), const(g), const(w_main), const(w_kidx2),
                  const(w_widx), const(inv64), const(inv128)],
        out_specs=[o[1] for o in outs],
        out_shape=[o[0] for o in outs],
        compiler_params=_cparams(("parallel",)),
        name="inproj",
    )(x, g, w_main, w_kidx2, w_widx, inv64, inv128)


def _lambda_value(lam_ref, lam_init):
    lp = lam_ref[...]
    a = jnp.sum(lp[0:1] * lp[1:2], axis=-1, keepdims=True)
    b = jnp.sum(lp[2:3] * lp[3:4], axis=-1, keepdims=True)
    return jnp.exp(a) - jnp.exp(b) + lam_init


def _diff_prompt_body(q_ref, k_ref, vt_ref, lam_ref, ght_ref, o_ref, s_ref, p_ref, *, tq, nt, nh, lam_init):
    i = pl.program_id(2)
    lane = lax.broadcasted_iota(jnp.int32, (1, LANES), 1)
    nq = 2 * tq
    kr = lax.broadcasted_iota(jnp.int32, (tq, tq), 0)
    qc = lax.broadcasted_iota(jnp.int32, (tq, tq), 1)
    bias = jnp.where(kr <= qc, 0.0, NEG_BIG).astype(F32)
    bias = jnp.concatenate([bias, bias], axis=1)
    lam = _lambda_value(lam_ref, lam_init)

    def attend(n):
        keys = n * tq
        for hh in range(nh):
            cols = slice(hh * DV_A, (hh + 1) * DV_A)
            q = q_ref[0, :, cols]
            zero = jnp.zeros_like(q)
            qq = jnp.concatenate([jnp.where(lane < D_A, q, zero), jnp.where(lane >= D_A, q, zero)], axis=0)
            s_ref[hh, 0:keys, :] = lax.dot_general(k_ref[0, 0:keys, cols], qq, NT_DIMS,
                                                   preferred_element_type=F32)
            mx = jnp.full((1, nq), NEG_BIG, F32)
            for j in range(n):
                blk = s_ref[hh, j * tq:(j + 1) * tq, :]
                if j == n - 1:
                    blk = blk + bias
                    s_ref[hh, j * tq:(j + 1) * tq, :] = blk
                mx = jnp.maximum(mx, jnp.max(blk, axis=0, keepdims=True))
            l = jnp.zeros((1, nq), F32)
            for j in range(n):
                p = jnp.exp(s_ref[hh, j * tq:(j + 1) * tq, :] - mx)
                p_ref[hh, j * tq:(j + 1) * tq, :] = p.astype(BF16)
                l = l + jnp.sum(p, axis=0, keepdims=True)
            acc = jnp.dot(vt_ref[hh, :, 0:keys], p_ref[hh, 0:keys, :], preferred_element_type=F32)
            o = acc * (1.0 / l)
            o = o[:, :tq] - lam * o[:, tq:]
            o = o * lax.rsqrt(jnp.mean(o * o, axis=0, keepdims=True) + EPS) * (ght_ref[...] * (1.0 - lam_init))
            o_ref[0, :, cols] = o.T.astype(o_ref.dtype)

    for n in range(1, nt + 1):
        pl.when(i == n - 1)(functools.partial(attend, n))


def _diff_prompt(qa, ka, va_t, lam_params, g_head_t, *, lam_init):
    b, s, _ = qa.shape
    tq = KV_TILE
    nt = s // tq
    nh = DIFF_HEADS_PER_STEP
    assert s % tq == 0 and va_t.shape == (H_A, DV_A, b * s) and H_A % nh == 0
    w = nh * DV_A
    return pl.pallas_call(
        functools.partial(_diff_prompt_body, tq=tq, nt=nt, nh=nh, lam_init=lam_init),
        grid=(b, H_A // nh, nt),
        in_specs=[
            pl.BlockSpec((1, tq, w), lambda bb, h, i: (bb, i, h)),
            pl.BlockSpec((1, s, w), lambda bb, h, i: (bb, 0, h)),
            pl.BlockSpec((nh, DV_A, s), lambda bb, h, i: (h, 0, bb)),
            pl.BlockSpec(lam_params.shape, lambda bb, h, i: (0, 0)),
            pl.BlockSpec(g_head_t.shape, lambda bb, h, i: (0, 0)),
        ],
        out_specs=pl.BlockSpec((1, tq, w), lambda bb, h, i: (bb, i, h)),
        out_shape=jax.ShapeDtypeStruct((b, s, H_A * DV_A), BF16),
        scratch_shapes=[pltpu.VMEM((nh, s, 2 * tq), F32), pltpu.VMEM((nh, s, 2 * tq), BF16)],
        compiler_params=_cparams(("parallel", "parallel", "parallel")),
        name="diff_prompt",
    )(qa, ka, va_t, lam_params, g_head_t)


def _dsa_prompt_body(qb_ref, qi_ref, wit_ref, kd_ref, kb_ref, vbt_ref, o_ref,
                     key_ref, qs_ref, qg_ref, bias_ref, s_ref, p_ref, *, tq, tk, nt, topk, idx_bits):
    i = pl.program_id(1)
    nkv = (i * tq + tq - 1) // tk + 1
    lane = lax.broadcasted_iota(jnp.int32, (1, LANES), 1)
    krow = lax.broadcasted_iota(jnp.int32, (tk, tq), 0)
    qcol = lax.broadcasted_iota(jnp.int32, (tk, tq), 1) + i * tq

    for h in range(H_IDX):
        blk = qi_ref[0, :, (h // 2) * LANES:(h // 2 + 1) * LANES]
        keep = (lane >= D_IDX) if h % 2 else (lane < D_IDX)
        qs_ref[h * tq:(h + 1) * tq, :] = jnp.where(keep, blk, jnp.zeros_like(blk))
    wt = wit_ref[...]

    def score_block(j, _):
        start = pl.multiple_of(j * tk, tk)
        r = lax.dot_general(kd_ref[0, pl.ds(start, tk), :], qs_ref[...], NT_DIMS,
                            preferred_element_type=F32)
        acc = jnp.zeros((tk, tq), F32)
        for h in range(H_IDX):
            acc = acc + wt[h:h + 1, :] * jnp.maximum(r[:, h * tq:(h + 1) * tq], 0.0)
        acc = jnp.where(krow + j * tk <= qcol, acc, -jnp.inf)
        key_ref[j] = _sort_key(acc)
        return 0

    lax.fori_loop(0, nkv, score_block, 0)

    def count(pred_fn):
        def blk(j, part):
            hit = jnp.where(pred_fn(key_ref[j], j), 1.0, 0.0)
            parts = [hit[r:r + 8] for r in range(0, tk, 8)]
            while len(parts) > 1:
                parts = [parts[a] + parts[a + 1] for a in range(0, len(parts), 2)]
            return part + parts[0]
        part = lax.fori_loop(0, nkv, blk, jnp.zeros((8, tq), F32))
        return jnp.sum(part, axis=0, keepdims=True).astype(jnp.int32)

    like = jnp.zeros((1, tq), jnp.int32)
    thr = _kth_largest_key(lambda c: count(lambda kk, j: kk >= c), like, topk)
    need = topk - count(lambda kk, j: kk > thr)
    n_eq = count(lambda kk, j: kk == thr)
    has_ties = jnp.max(jnp.where(n_eq != need, 1.0, 0.0)) > 0.5
    jbound = lax.cond(
        has_ties,
        lambda: _tie_index_bound(
            lambda c: count(lambda kk, j: (kk == thr) & (krow + j * tk < c)), need, like, idx_bits),
        lambda: jnp.full((1, tq), 2 ** idx_bits, jnp.int32))

    scale2 = (DH_B ** -0.5) * LOG2E
    group = H_B // KV_B
    for g in range(KV_B):
        for hh in range(group):
            qg_ref[g, hh * tq:(hh + 1) * tq, :] = qb_ref[0, :, (g * group + hh) * DH_B:(g * group + hh + 1) * DH_B]

    def attend(n):
        keys = n * tk
        for j in range(n):
            kk = key_ref[j]
            kidx = krow + j * tk
            sel = ((kk > thr) | ((kk == thr) & (kidx < jbound))) & (kidx <= qcol)
            bias_ref[j * tk:(j + 1) * tk, :] = jnp.where(sel, 0.0, NEG_BIG).astype(F32)
        nq = group * tq
        for g in range(KV_B):
            s_ref[0:keys, :] = lax.dot_general(kb_ref[0, 0:keys, g * DH_B:(g + 1) * DH_B], qg_ref[g], NT_DIMS,
                                               preferred_element_type=F32)
            mx = jnp.full((1, nq), NEG_BIG, F32)
            for j in range(n):
                b1 = bias_ref[j * tk:(j + 1) * tk, :]
                blk = s_ref[j * tk:(j + 1) * tk, :] * scale2 + jnp.concatenate([b1] * group, axis=1)
                s_ref[j * tk:(j + 1) * tk, :] = blk
                mx = jnp.maximum(mx, jnp.max(blk, axis=0, keepdims=True))
            l = jnp.zeros((1, nq), F32)
            for j in range(n):
                p = jnp.exp2(s_ref[j * tk:(j + 1) * tk, :] - mx)
                p_ref[j * tk:(j + 1) * tk, :] = p.astype(BF16)
                l = l + jnp.sum(p, axis=0, keepdims=True)
            o = jnp.dot(vbt_ref[g, :, 0:keys], p_ref[0:keys, :], preferred_element_type=F32) * (1.0 / l)
            for hh in range(group):
                h = g * group + hh
                o_ref[0, :, h * DH_B:(h + 1) * DH_B] = o[:, hh * tq:(hh + 1) * tq].T.astype(o_ref.dtype)

    for n in range(1, nt + 1):
        pl.when(nkv == n)(functools.partial(attend, n))


def _dsa_prompt(qb, qi, wi_t, kid, kb, vb_t):
    b, s, _ = qb.shape
    tq = min(s, LANES)
    tk = KV_TILE
    nt = s // tk
    topk = min(TOPK, s // 4)
    assert s % tq == 0 and s % tk == 0 and tk >= topk and tk % tq == 0
    assert vb_t.shape == (KV_B, DH_B, b * s) and wi_t.shape == (H_IDX, b * s)
    idx_bits = int(math.ceil(math.log2(s))) + 1
    blk_q = lambda w: pl.BlockSpec((1, tq, w), lambda bb, i: (bb, i, 0))
    blk_s = lambda w: pl.BlockSpec((1, s, w), lambda bb, i: (bb, 0, 0))
    return pl.pallas_call(
        functools.partial(_dsa_prompt_body, tq=tq, tk=tk, nt=nt, topk=topk, idx_bits=idx_bits),
        grid=(b, s // tq),
        in_specs=[blk_q(W_QB), blk_q(W_QI),
                  pl.BlockSpec((H_IDX, tq), lambda bb, i: (0, bb * (s // tq) + i)),
                  blk_s(2 * D_IDX), blk_s(W_KB),
                  pl.BlockSpec((KV_B, DH_B, s), lambda bb, i: (0, 0, bb))],
        out_specs=blk_q(W_QB),
        out_shape=jax.ShapeDtypeStruct((b, s, W_QB), BF16),
        scratch_shapes=[
            pltpu.VMEM((nt, tk, tq), jnp.int32),
            pltpu.VMEM((H_IDX * tq, LANES), BF16),
            pltpu.VMEM((KV_B, (H_B // KV_B) * tq, DH_B), BF16),
            pltpu.VMEM((s, tq), F32),
            pltpu.VMEM((s, (H_B // KV_B) * tq), F32),
            pltpu.VMEM((s, (H_B // KV_B) * tq), BF16),
        ],
        compiler_params=_cparams(("parallel", "parallel")),
        name="dsa_prompt",
    )(qb, qi, wi_t, kid, kb, vb_t)


def _outproj_body(h_ref, oa_ref, ob_ref, wa_ref, wb_ref, o_ref):
    o_ref[...] = (h_ref[...]
                  + jnp.dot(oa_ref[...], wa_ref[...], preferred_element_type=F32)
                  + jnp.dot(ob_ref[...], wb_ref[...], preferred_element_type=F32))


def _outproj(h, oa, ob, w_a, w_b):
    m, d = h.shape
    tm = min(m, 512)
    assert m % tm == 0
    row = lambda w: pl.BlockSpec((tm, w), lambda i: (i, 0))
    const = lambda a: pl.BlockSpec(a.shape, lambda i: (0, 0), pipeline_mode=pl.Buffered(1))
    return pl.pallas_call(
        _outproj_body,
        grid=(m // tm,),
        in_specs=[row(d), row(oa.shape[1]), row(ob.shape[1]), const(w_a), const(w_b)],
        out_specs=row(d),
        out_shape=jax.ShapeDtypeStruct((m, d), F32),
        compiler_params=_cparams(("parallel",)),
        name="outproj",
    )(h, oa, ob, w_a, w_b)


def _diff_decode_body(pt_ref, q_ref, ks_ref, vs_ref, lam_ref, gh_ref, *rest, pp, lam_init):
    k_refs = rest[:pp]
    v_refs = rest[pp:2 * pp]
    o_ref, m_ref, l_ref, acc_ref = rest[2 * pp:]
    j = pl.program_id(1)
    q = q_ref[0]
    lane = lax.broadcasted_iota(jnp.int32, (1, LANES), 1)
    zero = jnp.zeros_like(q)
    qq = jnp.concatenate([jnp.where(lane < D_A, q, zero), jnp.where(lane >= D_A, q, zero)], axis=0)
    rows_per_page = PAGE_SIZE * H_A

    @pl.when(j == 0)
    def _():
        ks = ks_ref[0].astype(BF16).astype(F32)
        prod = q.astype(F32) * ks
        s1 = jnp.sum(jnp.where(lane < D_A, prod, 0.0), axis=-1, keepdims=True)
        s2 = jnp.sum(jnp.where(lane >= D_A, prod, 0.0), axis=-1, keepdims=True)
        m_ref[...] = jnp.concatenate([s1, s2], axis=0)
        l_ref[...] = jnp.ones_like(l_ref)
        vs = vs_ref[0].astype(BF16).astype(F32)
        acc_ref[...] = jnp.concatenate([vs, vs], axis=0)

    head_of_row = lax.broadcasted_iota(jnp.int32, (2 * H_A, rows_per_page), 0) % H_A
    head_of_lane = lax.broadcasted_iota(jnp.int32, (2 * H_A, rows_per_page), 1) % H_A
    bias = jnp.where(head_of_row == head_of_lane, 0.0, NEG_BIG).astype(F32)
    s_all = []
    for p in range(pp):
        kp = k_refs[p][0].astype(BF16)
        s_all.append(lax.dot_general(qq, kp, NT_DIMS, preferred_element_type=F32) + bias)
    m = m_ref[...]
    m_new = m
    for s in s_all:
        m_new = jnp.maximum(m_new, jnp.max(s, axis=-1, keepdims=True))
    alpha = jnp.exp(m - m_new)
    l = alpha * l_ref[...]
    acc = alpha * acc_ref[...]
    for p in range(pp):
        pr = jnp.exp(s_all[p] - m_new)
        l = l + jnp.sum(pr, axis=-1, keepdims=True)
        acc = acc + jnp.dot(pr.astype(BF16), v_refs[p][0].astype(BF16), preferred_element_type=F32)
    m_ref[...] = m_new
    l_ref[...] = l
    acc_ref[...] = acc

    @pl.when(j == pl.num_programs(1) - 1)
    def _():
        o = acc / l
        lam = _lambda_value(lam_ref, lam_init)
        o = o[:H_A] - lam * o[H_A:]
        o_ref[0] = (_rms(o, gh_ref[...]) * (1.0 - lam_init)).astype(o_ref.dtype)


def _diff_decode(page_table, qa, k_self, v_self, lam_params, g_head, cache_k, cache_v, *, lam_init):
    db, n_pages = page_table.shape
    pp = 8 if n_pages % 8 == 0 else 1
    rows = PAGE_SIZE * H_A
    tok = pl.BlockSpec((1, H_A, DV_A), lambda b, j, pt: (b, 0, 0))
    page = lambda p: pl.BlockSpec((1, rows, DV_A), lambda b, j, pt: (pt[b, j * pp + p], 0, 0))
    grid_spec = pltpu.PrefetchScalarGridSpec(
        num_scalar_prefetch=1,
        grid=(db, n_pages // pp),
        in_specs=[tok, tok, tok,
                  pl.BlockSpec(lam_params.shape, lambda b, j, pt: (0, 0)),
                  pl.BlockSpec(g_head.shape, lambda b, j, pt: (0, 0))]
                 + [page(p) for p in range(pp)] + [page(p) for p in range(pp)],
        out_specs=tok,
        scratch_shapes=[pltpu.VMEM((2 * H_A, 1), F32), pltpu.VMEM((2 * H_A, 1), F32),
                        pltpu.VMEM((2 * H_A, DV_A), F32)],
    )
    return pl.pallas_call(
        functools.partial(_diff_decode_body, pp=pp, lam_init=lam_init),
        grid_spec=grid_spec,
        out_shape=jax.ShapeDtypeStruct((db, H_A, DV_A), BF16),
        compiler_params=_cparams(("parallel", "arbitrary")),
        name="diff_decode",
    )(page_table, qa, k_self, v_self, lam_params, g_head, *([cache_k] * pp), *([cache_v] * pp))


def _idx_decode_body(pt_ref, qs_ref, w_ref, kself_ref, *rest, pp, n_steps):
    k_refs = rest[:pp]
    o_ref = rest[pp]
    j = pl.program_id(1)
    qs = qs_ref[0]
    w = w_ref[0]

    @pl.when(j < n_steps)
    def _():
        for p in range(pp):
            kp = k_refs[p][0].astype(BF16)
            r = jnp.dot(qs, kp, preferred_element_type=F32)
            o_ref[0, p:p + 1, :] = jnp.sum(w * jnp.maximum(r, 0.0), axis=0, keepdims=True)

    @pl.when(j == n_steps)
    def _():
        ks = kself_ref[0].astype(BF16).astype(F32)
        r = jnp.sum(qs.astype(F32) * ks, axis=-1, keepdims=True)
        sc = jnp.sum(w * jnp.maximum(r, 0.0), axis=0, keepdims=True)
        lane = lax.broadcasted_iota(jnp.int32, (1, LANES), 1)
        o_ref[0] = jnp.full((pp, LANES), -jnp.inf, F32)
        o_ref[0, 0:1, :] = jnp.where(lane == 0, sc, -jnp.inf)


def _idx_decode(page_table, qs, w, k_self, cache_idx_t):
    db, n_pages = page_table.shape
    pp = min(IDX_PAGES_PER_STEP, n_pages)
    assert n_pages % pp == 0
    n_steps = n_pages // pp
    page = lambda p: pl.BlockSpec(
        (1, D_IDX, PAGE_SIZE),
        lambda b, j, pt: (pt[b, jnp.minimum(j * pp + p, n_pages - 1)], 0, 0))
    grid_spec = pltpu.PrefetchScalarGridSpec(
        num_scalar_prefetch=1,
        grid=(db, n_steps + 1),
        in_specs=[pl.BlockSpec((1, H_IDX, D_IDX), lambda b, j, pt: (b, 0, 0)),
                  pl.BlockSpec((1, H_IDX, 1), lambda b, j, pt: (b, 0, 0)),
                  pl.BlockSpec((1, 1, D_IDX), lambda b, j, pt: (b, 0, 0))]
                 + [page(p) for p in range(pp)],
        out_specs=pl.BlockSpec((1, pp, LANES), lambda b, j, pt: (b, j, 0)),
    )
    return pl.pallas_call(
        functools.partial(_idx_decode_body, pp=pp, n_steps=n_steps),
        grid_spec=grid_spec,
        out_shape=jax.ShapeDtypeStruct((db, n_pages + pp, LANES), F32),
        compiler_params=_cparams(("parallel", "arbitrary")),
        name="idx_decode",
    )(page_table, qs, w, k_self, *([cache_idx_t] * pp))


def _select_decode_body(s_ref, thr_ref, jb_ref, *, topk, idx_bits):
    keys = _sort_key(s_ref[...])
    n = keys.shape[1]
    col = lax.broadcasted_iota(jnp.int32, keys.shape, 1)

    def row_count(pred):
        hit = jnp.where(pred, 1.0, 0.0)
        part = jnp.zeros((keys.shape[0], LANES), F32)
        for c in range(0, n, LANES):
            part = part + hit[:, c:c + LANES]
        return jnp.sum(part, axis=-1, keepdims=True).astype(jnp.int32)

    like = jnp.zeros((keys.shape[0], 1), jnp.int32)
    thr = _kth_largest_key(lambda c: row_count(keys >= c), like, topk)
    need = topk - row_count(keys > thr)
    jb = _tie_index_bound(lambda c: row_count((keys == thr) & (col < c)), need, like, idx_bits)
    thr_ref[...] = thr
    jb_ref[...] = jb


def _select_decode(scores, topk):
    db, n = scores.shape
    idx_bits = int(math.ceil(math.log2(n))) + 1
    return pl.pallas_call(
        functools.partial(_select_decode_body, topk=topk, idx_bits=idx_bits),
        out_shape=[jax.ShapeDtypeStruct((db, 1), jnp.int32)] * 2,
        compiler_params=pltpu.CompilerParams(vmem_limit_bytes=VMEM_LIMIT),
        name="select_decode",
    )(scores)


def _dsa_decode_body(pt_ref, thr_ref, jb_ref, q_ref, ks_ref, vs_ref, sc_ref, *rest, pp, n_steps):
    k_refs = rest[:pp]
    v_refs = rest[pp:2 * pp]
    o_ref, m_ref, l_ref, acc_ref = rest[2 * pp:]
    b = pl.program_id(0)
    j = pl.program_id(1)
    q = q_ref[0]
    thr = thr_ref[b]
    jb = jb_ref[b]
    scale = DH_B ** -0.5
    group = H_B // KV_B
    rows_per_page = PAGE_SIZE * KV_B

    @pl.when(j == 0)
    def _():
        m_ref[...] = jnp.full_like(m_ref, NEG_BIG)
        l_ref[...] = jnp.zeros_like(l_ref)
        acc_ref[...] = jnp.zeros_like(acc_ref)

    keys = _sort_key(sc_ref[0])
    idx = (lax.broadcasted_iota(jnp.int32, (pp, LANES), 0) + j * pp) * PAGE_SIZE \
        + lax.broadcasted_iota(jnp.int32, (pp, LANES), 1)
    sel = (keys > thr) | ((keys == thr) & (idx < jb))

    @pl.when(j < n_steps)
    def _():
        spread = (lax.broadcasted_iota(jnp.int32, (LANES, rows_per_page), 1) // KV_B
                  == lax.broadcasted_iota(jnp.int32, (LANES, rows_per_page), 0))
        sel2 = jnp.dot(jnp.where(sel, 1.0, 0.0).astype(BF16), jnp.where(spread, 1.0, 0.0).astype(BF16),
                       preferred_element_type=F32) > 0.5
        kv_of_row = lax.broadcasted_iota(jnp.int32, (H_B, rows_per_page), 0) // group
        kv_of_lane = lax.broadcasted_iota(jnp.int32, (H_B, rows_per_page), 1) % KV_B
        same_kv = kv_of_row == kv_of_lane
        s_all, ok_all = [], []
        for p in range(pp):
            kp = k_refs[p][0].astype(BF16)
            s = lax.dot_general(q, kp, NT_DIMS, preferred_element_type=F32) * scale
            ok = same_kv & sel2[p:p + 1, :]
            s_all.append(jnp.where(ok, s, NEG_BIG))
            ok_all.append(ok)
        m = m_ref[...]
        m_new = m
        for s in s_all:
            m_new = jnp.maximum(m_new, jnp.max(s, axis=-1, keepdims=True))
        alpha = jnp.exp(m - m_new)
        l = alpha * l_ref[...]
        acc = alpha * acc_ref[...]
        for p in range(pp):
            pr = jnp.where(ok_all[p], jnp.exp(s_all[p] - m_new), 0.0)
            l = l + jnp.sum(pr, axis=-1, keepdims=True)
            acc = acc + jnp.dot(pr.astype(BF16), v_refs[p][0].astype(BF16), preferred_element_type=F32)
        m_ref[...] = m_new
        l_ref[...] = l
        acc_ref[...] = acc

    @pl.when(j == n_steps)
    def _():
        row8 = lax.broadcasted_iota(jnp.int32, (H_B, LANES), 0)
        ks = ks_ref[0].astype(BF16).astype(F32)
        vs = vs_ref[0].astype(BF16).astype(F32)
        kexp = jnp.where(row8 < group, ks[0:1], ks[1:2])
        vexp = jnp.where(row8 < group, vs[0:1], vs[1:2])
        s = jnp.sum(q.astype(F32) * kexp, axis=-1, keepdims=True) * scale
        ok = sel[0:1, 0:1]
        s = jnp.where(ok, s, NEG_BIG)
        m = m_ref[...]
        m_new = jnp.maximum(m, s)
        alpha = jnp.exp(m - m_new)
        pr = jnp.where(ok, jnp.exp(s - m_new), 0.0)
        l = alpha * l_ref[...] + pr
        acc = alpha * acc_ref[...] + pr * vexp
        o_ref[0] = (acc / l).astype(o_ref.dtype)


def _dsa_decode(page_table, thr, jb, qb, k_self, v_self, scores3, cache_k, cache_v):
    db, n_pages = page_table.shape
    pp = min(DSA_PAGES_PER_STEP, n_pages)
    assert n_pages % pp == 0 and scores3.shape[1] >= n_pages + pp
    n_steps = n_pages // pp
    rows = PAGE_SIZE * KV_B
    page = lambda p: pl.BlockSpec(
        (1, rows, DH_B),
        lambda b, j, pt, t, jj: (pt[b, jnp.minimum(j * pp + p, n_pages - 1)], 0, 0))
    tok = lambda r: pl.BlockSpec((1, r, DH_B), lambda b, j, pt, t, jj: (b, 0, 0))
    grid_spec = pltpu.PrefetchScalarGridSpec(
        num_scalar_prefetch=3,
        grid=(db, n_steps + 1),
        in_specs=[tok(H_B), tok(KV_B), tok(KV_B),
                  pl.BlockSpec((1, pp, LANES), lambda b, j, pt, t, jj: (b, j, 0))]
                 + [page(p) for p in range(pp)] + [page(p) for p in range(pp)],
        out_specs=tok(H_B),
        scratch_shapes=[pltpu.VMEM((H_B, 1), F32), pltpu.VMEM((H_B, 1), F32),
                        pltpu.VMEM((H_B, DH_B), F32)],
    )
    return pl.pallas_call(
        functools.partial(_dsa_decode_body, pp=pp, n_steps=n_steps),
        grid_spec=grid_spec,
        out_shape=jax.ShapeDtypeStruct((db, H_B, DH_B), BF16),
        compiler_params=_cparams(("parallel", "arbitrary")),
        name="dsa_decode",
    )(page_table, thr, jb, qb, k_self, v_self, scores3, *([cache_k] * pp), *([cache_v] * pp))


def _rope_inv(half):
    return ROPE_THETA ** (-jnp.arange(half, dtype=F32) / half)


def kernel(x_prompt, x_sample, cache_diff_k, cache_diff_v, cache_dsa_k, cache_dsa_v, cache_idx_k,
           page_table, norm_ffn1, w_ffn1_in, w_ffn1_out, norm_mix, w_in, lambda_q1, lambda_k1,
           lambda_q2, lambda_k2, norm_head, w_out, norm_ffn2, w_ffn2_in, w_ffn2_out, norm_final):
    depth = w_in.shape[0]
    assert depth == 1 and x_sample.shape[1] == 1
    bsz, seq, d = x_prompt.shape
    db = x_sample.shape[0]
    n_phys = cache_diff_k.shape[1]
    n_pages = page_table.shape[1]
    past = n_pages * PAGE_SIZE
    lam_init = 0.8 - 0.6 * math.exp(-0.3 * 0)

    w1i, w1o = w_ffn1_in[0].astype(BF16), w_ffn1_out[0].astype(BF16)
    w2i, w2o = w_ffn2_in[0].astype(BF16), w_ffn2_out[0].astype(BF16)
    wi_all = w_in[0].astype(BF16)
    w_main = wi_all[:, :W_MAIN]
    w_k = wi_all[:, W_MAIN:W_MAIN + D_IDX]
    w_kidx2 = jnp.concatenate([w_k, w_k], axis=1)
    w_widx = jnp.pad(wi_all[:, W_MAIN + D_IDX:], ((0, 0), (0, LANES - H_IDX)))
    wo_all = w_out[0].astype(BF16)
    wo_a, wo_b = wo_all[:H_A * DV_A], wo_all[H_A * DV_A:]
    g1, gm, g2 = norm_ffn1[0][None], norm_mix[0][None], norm_ffn2[0][None]
    gh, gf = norm_head[0][None], norm_final[None]
    lam_params = jnp.stack([lambda_q1[0], lambda_k1[0], lambda_q2[0], lambda_k2[0]]).astype(F32)
    inv32, inv64h = _rope_inv(D_A // 2), _rope_inv(DH_B // 2)
    inv64 = jnp.tile(inv32, LANES // (D_A // 2))[None]
    inv128 = jnp.tile(inv64h, LANES // (DH_B // 2))[None]

    def front(x2d, pos_offset, period, prompt):
        h = _ffn(x2d, g1, w1i, w1o, gf, final_norm=False)
        return h, _inproj(h, gm, w_main, w_kidx2, w_widx, inv64, inv128,
                          pos_offset=pos_offset, period=period, prompt=prompt)

    def back(h, oa, ob):
        h = _outproj(h, oa, ob, wo_a, wo_b)
        return _ffn(h, g2, w2i, w2o, gf, final_norm=True)

    mp = bsz * seq
    hp, (qa, ka, kab, va, vat, qb, kb, kbb, vb, vbt, qi, ki, kid, wit) = front(
        x_prompt.reshape(mp, d), 0, seq, True)
    r3 = lambda a: a.reshape(bsz, seq, a.shape[-1])
    oa = _diff_prompt(r3(qa), r3(kab), vat, lam_params, gh.T, lam_init=lam_init)
    ob = _dsa_prompt(r3(qb), r3(qi), wit, r3(kid), r3(kbb), vbt)
    y_prompt = back(hp, oa.reshape(mp, -1), ob.reshape(mp, -1)).reshape(bsz, seq, d)

    hs, (qa_s, ka_s, va_s, qb_s, kb_s, vb_s, qi_s, ki_s, wi_s) = front(
        x_sample.reshape(db, d), past, 1, False)
    oa_s = _diff_decode(
        page_table, qa_s.reshape(db, H_A, DV_A), ka_s.reshape(db, H_A, DV_A),
        va_s.reshape(db, H_A, DV_A), lam_params, gh,
        cache_diff_k.reshape(n_phys, PAGE_SIZE * H_A, DV_A),
        cache_diff_v.reshape(n_phys, PAGE_SIZE * H_A, DV_A), lam_init=lam_init)
    scores3 = _idx_decode(page_table, qi_s.reshape(db, H_IDX, D_IDX), wi_s.reshape(db, H_IDX, 1),
                          ki_s.reshape(db, 1, D_IDX),
                          jnp.swapaxes(cache_idx_k.reshape(n_phys, PAGE_SIZE, D_IDX), 1, 2))
    thr, jb = _select_decode(scores3.reshape(db, -1), min(TOPK, (past + 1) // 4))
    ob_s = _dsa_decode(
        page_table, thr.reshape(db), jb.reshape(db), qb_s.reshape(db, H_B, DH_B),
        kb_s.reshape(db, KV_B, DH_B), vb_s.reshape(db, KV_B, DH_B), scores3,
        cache_dsa_k.reshape(n_phys, PAGE_SIZE * KV_B, DH_B),
        cache_dsa_v.reshape(n_phys, PAGE_SIZE * KV_B, DH_B))
    y_sample = back(hs, oa_s.reshape(db, -1), ob_s.reshape(db, -1)).reshape(db, 1, d)

    p5 = lambda a, h, w: a.reshape(1, bsz, seq, h, w)
    s5 = lambda a, h, w: a.reshape(1, db, 1, h, w)
    return (y_prompt, y_sample,
            p5(ka, H_A, 2 * D_A), p5(va, H_A, DV_A), p5(kb, KV_B, DH_B), p5(vb, KV_B, DH_B),
            ki.reshape(1, bsz, seq, D_IDX),
            s5(ka_s, H_A, 2 * D_A), s5(va_s, H_A, DV_A), s5(kb_s, KV_B, DH_B), s5(vb_s, KV_B, DH_B),
            ki_s.reshape(1, db, 1, D_IDX))
```

```python
import functools
import math

import jax
import jax.numpy as jnp
from jax import lax
from jax.experimental import pallas as pl
from jax.experimental.pallas import tpu as pltpu

H_A = 8
D_A = 64
DV_A = 2 * D_A
H_B = 8
KV_B = 2
DH_B = 128
H_IDX = 16
D_IDX = 64
TOPK = 256
ROPE_THETA = 10000.0
EPS = 1e-6
PAGE_SIZE = 128

LANES = 128
FFN_ROWS = 1024
DIFF_HEADS_PER_STEP = 4
KV_TILE = 256
LOG2E = 1.4426950408889634
IDX_PAGES_PER_STEP = 32
DSA_PAGES_PER_STEP = 16
NEG_BIG = -1e30
INT_MIN = -(2 ** 31)
VMEM_LIMIT = 56 * 1024 * 1024

W_QA = H_A * 2 * D_A
W_KA = H_A * 2 * D_A
W_VA = H_A * DV_A
W_QB = H_B * DH_B
W_KB = KV_B * DH_B
W_VB = KV_B * DH_B
W_QI = H_IDX * D_IDX
W_MAIN = W_QA + W_KA + W_VA + W_QB + W_KB + W_VB + W_QI

F32 = jnp.float32
BF16 = jnp.bfloat16
NT_DIMS = (((1,), (1,)), ((), ()))


def _cparams(sem):
    return pltpu.CompilerParams(dimension_semantics=sem, vmem_limit_bytes=VMEM_LIMIT)


def _rms(x, g):
    return x * lax.rsqrt(jnp.mean(x * x, axis=-1, keepdims=True) + EPS) * g


def _sort_key(x):
    b = lax.bitcast_convert_type(x, jnp.int32)
    return b ^ ((b >> 31) & jnp.int32(0x7FFFFFFF))


def _kth_largest_key(count_ge, like, k):
    def body(it, t):
        c = t | lax.shift_left(jnp.int32(1), 31 - it)
        cnt = count_ge(c ^ jnp.int32(INT_MIN))
        return jnp.where(cnt >= k, c, t)

    t = lax.fori_loop(0, 32, body, jnp.zeros_like(like))
    return t ^ jnp.int32(INT_MIN)


def _tie_index_bound(count_eq_below, need, like, nbits):
    def body(it, jb):
        c = jb | lax.shift_left(jnp.int32(1), nbits - 1 - it)
        return jnp.where(count_eq_below(c) <= need, c, jb)

    return lax.fori_loop(0, nbits, body, jnp.zeros_like(like))


def _ffn_body(x_ref, g_ref, wg_ref, wu_ref, wo_ref, gf_ref, o_ref, xn_ref, *, final_norm):
    j = pl.program_id(1)

    @pl.when(j == 0)
    def _():
        x = x_ref[...]
        xn_ref[...] = _rms(x, g_ref[...]).astype(BF16)
        o_ref[...] = x

    xn = xn_ref[...]
    gate = jnp.dot(xn, wg_ref[...], preferred_element_type=F32)
    up = jnp.dot(xn, wu_ref[...], preferred_element_type=F32)
    act = (gate * jax.nn.sigmoid(gate) * (0.5 * up)).astype(BF16)
    o_ref[...] += jnp.dot(act, wo_ref[...], preferred_element_type=F32)

    if final_norm:
        @pl.when(j == pl.num_programs(1) - 1)
        def _():
            o_ref[...] = _rms(o_ref[...], gf_ref[...])


def _ffn(x, g, w_in, w_out, g_final, *, final_norm):
    m, d = x.shape
    f = w_out.shape[0]
    tm = min(m, FFN_ROWS)
    tf = min(f, 512)
    assert m % tm == 0 and f % tf == 0
    nf = f // tf
    return pl.pallas_call(
        functools.partial(_ffn_body, final_norm=final_norm),
        grid=(m // tm, nf),
        in_specs=[
            pl.BlockSpec((tm, d), lambda i, j: (i, 0)),
            pl.BlockSpec((1, d), lambda i, j: (0, 0)),
            pl.BlockSpec((d, tf), lambda i, j: (0, j)),
            pl.BlockSpec((d, tf), lambda i, j: (0, j + nf)),
            pl.BlockSpec((tf, d), lambda i, j: (j, 0)),
            pl.BlockSpec((1, d), lambda i, j: (0, 0)),
        ],
        out_specs=pl.BlockSpec((tm, d), lambda i, j: (i, 0)),
        out_shape=jax.ShapeDtypeStruct((m, d), F32),
        scratch_shapes=[pltpu.VMEM((tm, d), BF16)],
        compiler_params=_cparams(("parallel", "arbitrary")),
        name="ffn_final" if final_norm else "ffn",
    )(x, g, w_in, w_in, w_out, g_final)


def _inproj_body(x_ref, g_ref, wm_ref, wk_ref, ww_ref, inv64_ref, inv128_ref, *out_refs,
                 tm, pos_offset, period, prompt):
    if prompt:
        (qa_ref, ka_ref, kab_ref, va_ref, vat_ref, qb_ref, kb_ref, kbb_ref, vb_ref, vbt_ref,
         qi_ref, ki_ref, kid_ref, wi_ref) = out_refs
    else:
        qa_ref, ka_ref, va_ref, qb_ref, kb_ref, vb_ref, qi_ref, ki_ref, wi_ref = out_refs
    xn = _rms(x_ref[...], g_ref[...]).astype(BF16)
    if period == 1:
        pos = jnp.full((tm, 1), float(pos_offset), F32)
    else:
        row0 = (pl.program_id(0) * tm) % period
        pos = (pos_offset + row0 + lax.broadcasted_iota(jnp.int32, (tm, 1), 0)).astype(F32)
    lane = lax.broadcasted_iota(jnp.int32, (1, LANES), 1)
    lo64 = (lane % 64) < 32
    lo128 = lane < 64
    ang64 = pos * inv64_ref[...]
    ang128 = pos * inv128_ref[...]
    cos64 = jnp.cos(ang64)
    sin64 = jnp.where(lo64, -jnp.sin(ang64), jnp.sin(ang64))
    cos128 = jnp.cos(ang128)
    sin128 = jnp.where(lo128, -jnp.sin(ang128), jnp.sin(ang128))

    def rope64(z):
        rot = jnp.where(lo64, pltpu.roll(z, 96, 1), pltpu.roll(z, 32, 1))
        return z * cos64 + rot * sin64

    def rope128(z):
        return z * cos128 + pltpu.roll(z, 64, 1) * sin128

    def rows(ref):
        def put(col, zc):
            ref[:, col:col + LANES] = zc.astype(ref.dtype)
        return put

    def tiles(ref):
        def put(col, zc):
            ref[col // LANES] = zc.T.astype(ref.dtype)
        return put

    def region(start, width, fn, sinks):
        step = 512 if width % 512 == 0 else width
        for c0 in range(0, width, step):
            z = jnp.dot(xn, wm_ref[:, start + c0:start + c0 + step], preferred_element_type=F32)
            for c in range(0, step, LANES):
                zc = fn(z[:, c:c + LANES])
                for put in sinks:
                    put(c0 + c, zc)

    ident = lambda z: z
    off = 0
    region(off, W_QA, lambda z: rope64(z) * (D_A ** -0.5), [rows(qa_ref)]); off += W_QA
    region(off, W_KA, rope64, [rows(ka_ref)] + ([rows(kab_ref)] if prompt else [])); off += W_KA
    region(off, W_VA, ident, [rows(va_ref)] + ([tiles(vat_ref)] if prompt else [])); off += W_VA
    region(off, W_QB, rope128, [rows(qb_ref)]); off += W_QB
    region(off, W_KB, rope128, [rows(kb_ref)] + ([rows(kbb_ref)] if prompt else [])); off += W_KB
    region(off, W_VB, ident, [rows(vb_ref)] + ([tiles(vbt_ref)] if prompt else [])); off += W_VB
    region(off, W_QI, lambda z: rope64(z) * (D_IDX ** -0.5), [rows(qi_ref)]); off += W_QI
    kd = rope64(jnp.dot(xn, wk_ref[...], preferred_element_type=F32))
    ki_ref[...] = kd[:, :D_IDX]
    wz = jnp.dot(xn, ww_ref[...], preferred_element_type=F32) * (H_IDX ** -0.5)
    if prompt:
        kid_ref[...] = kd.astype(BF16)
        wi_ref[...] = wz.T[:H_IDX]
    else:
        wi_ref[...] = wz[:, :H_IDX]


def _inproj(x, g, w_main, w_kidx2, w_widx, inv64, inv128, *, pos_offset, period, prompt):
    m, d = x.shape
    tm = min(m, KV_TILE)
    assert m % tm == 0 and (period == 1 or period % tm == 0)
    nt = m // tm
    row = lambda w, dt: (jax.ShapeDtypeStruct((m, w), dt), pl.BlockSpec((tm, w), lambda i: (i, 0)))
    tile = lambda h: (jax.ShapeDtypeStruct((h, LANES, m), BF16),
                      pl.BlockSpec((h, LANES, tm), lambda i: (0, 0, i)))
    const = lambda a: pl.BlockSpec(a.shape, lambda i: (0, 0), pipeline_mode=pl.Buffered(1))
    if prompt:
        outs = [row(W_QA, BF16), row(W_KA, F32), row(W_KA, BF16), row(W_VA, F32), tile(H_A),
                row(W_QB, BF16), row(W_KB, F32), row(W_KB, BF16), row(W_VB, F32), tile(KV_B),
                row(W_QI, BF16), row(D_IDX, F32), row(2 * D_IDX, BF16),
                (jax.ShapeDtypeStruct((H_IDX, m), F32), pl.BlockSpec((H_IDX, tm), lambda i: (0, i)))]
    else:
        outs = [row(W_QA, BF16), row(W_KA, F32), row(W_VA, F32), row(W_QB, BF16), row(W_KB, F32),
                row(W_VB, F32), row(W_QI, BF16), row(D_IDX, F32), row(H_IDX, F32)]
    return pl.pallas_call(
        functools.partial(_inproj_body, tm=tm, pos_offset=pos_offset, period=period, prompt=prompt),
        grid=(nt,),
        in_specs=[pl.BlockSpec((tm, d), lambda i: (i, 0)), const(g), const(w_main), const(w_kidx2),
                  const(w_widx), const(inv64), const(inv128)],
        out_specs=[o[1] for o in outs],
        out_shape=[o[0] for o in outs],
        compiler_params=_cparams(("parallel",)),
        name="inproj",
    )(x, g, w_main, w_kidx2, w_widx, inv64, inv128)


def _lambda_value(lam_ref, lam_init):
    lp = lam_ref[...]
    a = jnp.sum(lp[0:1] * lp[1:2], axis=-1, keepdims=True)
    b = jnp.sum(lp[2:3] * lp[3:4], axis=-1, keepdims=True)
    return jnp.exp(a) - jnp.exp(b) + lam_init


def _diff_prompt_body(q_ref, k_ref, vt_ref, lam_ref, ght_ref, o_ref, s_ref, p_ref, *, tq, nt, nh, lam_init):
    i = pl.program_id(2)
    lane = lax.broadcasted_iota(jnp.int32, (1, LANES), 1)
    nq = 2 * tq
    kr = lax.broadcasted_iota(jnp.int32, (tq, tq), 0)
    qc = lax.broadcasted_iota(jnp.int32, (tq, tq), 1)
    bias = jnp.where(kr <= qc, 0.0, NEG_BIG).astype(F32)
    bias = jnp.concatenate([bias, bias], axis=1)
    lam = _lambda_value(lam_ref, lam_init)

    def attend(n):
        keys = n * tq
        cols = [slice(hh * DV_A, (hh + 1) * DV_A) for hh in range(nh)]
        for hh in range(nh):
            q = q_ref[0, :, cols[hh]]
            zero = jnp.zeros_like(q)
            qq = jnp.concatenate([jnp.where(lane < D_A, q, zero), jnp.where(lane >= D_A, q, zero)], axis=0)
            s_ref[hh, 0:keys, :] = lax.dot_general(k_ref[0, 0:keys, cols[hh]], qq, NT_DIMS,
                                                   preferred_element_type=F32)
        for hh in range(nh):
            mx = jnp.full((1, nq), NEG_BIG, F32)
            for j in range(n):
                blk = s_ref[hh, j * tq:(j + 1) * tq, :]
                if j == n - 1:
                    blk = blk + bias
                    s_ref[hh, j * tq:(j + 1) * tq, :] = blk
                mx = jnp.maximum(mx, jnp.max(blk, axis=0, keepdims=True))
            l = jnp.zeros((1, nq), F32)
            for j in range(n):
                p = jnp.exp(s_ref[hh, j * tq:(j + 1) * tq, :] - mx)
                p_ref[hh, j * tq:(j + 1) * tq, :] = p.astype(BF16)
                l = l + jnp.sum(p, axis=0, keepdims=True)
            acc = jnp.dot(vt_ref[hh, :, 0:keys], p_ref[hh, 0:keys, :], preferred_element_type=F32)
            o = acc * (1.0 / l)
            o = o[:, :tq] - lam * o[:, tq:]
            o = o * lax.rsqrt(jnp.mean(o * o, axis=0, keepdims=True) + EPS) * (ght_ref[...] * (1.0 - lam_init))
            o_ref[0, :, cols[hh]] = o.T.astype(o_ref.dtype)

    for n in range(1, nt + 1):
        pl.when(i == n - 1)(functools.partial(attend, n))


def _diff_prompt(qa, ka, va_t, lam_params, g_head_t, *, lam_init):
    b, s, _ = qa.shape
    tq = KV_TILE
    nt = s // tq
    nh = DIFF_HEADS_PER_STEP
    assert s % tq == 0 and va_t.shape == (H_A, DV_A, b * s) and H_A % nh == 0
    w = nh * DV_A
    return pl.pallas_call(
        functools.partial(_diff_prompt_body, tq=tq, nt=nt, nh=nh, lam_init=lam_init),
        grid=(b, H_A // nh, nt),
        in_specs=[
            pl.BlockSpec((1, tq, w), lambda bb, h, i: (bb, i, h)),
            pl.BlockSpec((1, s, w), lambda bb, h, i: (bb, 0, h)),
            pl.BlockSpec((nh, DV_A, s), lambda bb, h, i: (h, 0, bb)),
            pl.BlockSpec(lam_params.shape, lambda bb, h, i: (0, 0)),
            pl.BlockSpec(g_head_t.shape, lambda bb, h, i: (0, 0)),
        ],
        out_specs=pl.BlockSpec((1, tq, w), lambda bb, h, i: (bb, i, h)),
        out_shape=jax.ShapeDtypeStruct((b, s, H_A * DV_A), BF16),
        scratch_shapes=[pltpu.VMEM((nh, s, 2 * tq), F32), pltpu.VMEM((nh, s, 2 * tq), BF16)],
        compiler_params=_cparams(("parallel", "parallel", "parallel")),
        name="diff_prompt",
    )(qa, ka, va_t, lam_params, g_head_t)


def _dsa_prompt_body(qb_ref, qi_ref, wit_ref, kd_ref, kb_ref, vbt_ref, o_ref,
                     key_ref, qs_ref, qg_ref, bias_ref, s_ref, p_ref, *, tq, tk, nt, topk, idx_bits):
    i = pl.program_id(1)
    nkv = (i * tq + tq - 1) // tk + 1
    lane = lax.broadcasted_iota(jnp.int32, (1, LANES), 1)
    krow = lax.broadcasted_iota(jnp.int32, (tk, tq), 0)
    qcol = lax.broadcasted_iota(jnp.int32, (tk, tq), 1) + i * tq

    for h in range(H_IDX):
        blk = qi_ref[0, :, (h // 2) * LANES:(h // 2 + 1) * LANES]
        keep = (lane >= D_IDX) if h % 2 else (lane < D_IDX)
        qs_ref[h * tq:(h + 1) * tq, :] = jnp.where(keep, blk, jnp.zeros_like(blk))
    wt = wit_ref[...]

    def score_block(j, _):
        start = pl.multiple_of(j * tk, tk)
        r = lax.dot_general(kd_ref[0, pl.ds(start, tk), :], qs_ref[...], NT_DIMS,
                            preferred_element_type=F32)
        acc = jnp.zeros((tk, tq), F32)
        for h in range(H_IDX):
            acc = acc + wt[h:h + 1, :] * jnp.maximum(r[:, h * tq:(h + 1) * tq], 0.0)
        acc = jnp.where(krow + j * tk <= qcol, acc, -jnp.inf)
        key_ref[j] = _sort_key(acc)
        return 0

    lax.fori_loop(0, nkv, score_block, 0)

    def count(pred_fn):
        def blk(j, part):
            hit = jnp.where(pred_fn(key_ref[j], j), 1.0, 0.0)
            parts = [hit[r:r + 8] for r in range(0, tk, 8)]
            while len(parts) > 1:
                parts = [parts[a] + parts[a + 1] for a in range(0, len(parts), 2)]
            return part + parts[0]
        part = lax.fori_loop(0, nkv, blk, jnp.zeros((8, tq), F32))
        return jnp.sum(part, axis=0, keepdims=True).astype(jnp.int32)

    like = jnp.zeros((1, tq), jnp.int32)
    thr = _kth_largest_key(lambda c: count(lambda kk, j: kk >= c), like, topk)
    need = topk - count(lambda kk, j: kk > thr)
    n_eq = count(lambda kk, j: kk == thr)
    has_ties = jnp.max(jnp.where(n_eq != need, 1.0, 0.0)) > 0.5
    jbound = lax.cond(
        has_ties,
        lambda: _tie_index_bound(
            lambda c: count(lambda kk, j: (kk == thr) & (krow + j * tk < c)), need, like, idx_bits),
        lambda: jnp.full((1, tq), 2 ** idx_bits, jnp.int32))

    scale2 = (DH_B ** -0.5) * LOG2E
    group = H_B // KV_B
    for g in range(KV_B):
        for hh in range(group):
            qg_ref[g, hh * tq:(hh + 1) * tq, :] = qb_ref[0, :, (g * group + hh) * DH_B:(g * group + hh + 1) * DH_B]

    def attend(n):
        keys = n * tk
        for j in range(n):
            kk = key_ref[j]
            kidx = krow + j * tk
            sel = ((kk > thr) | ((kk == thr) & (kidx < jbound))) & (kidx <= qcol)
            bias_ref[j * tk:(j + 1) * tk, :] = jnp.where(sel, 0.0, NEG_BIG).astype(F32)
        nq = group * tq
        for g in range(KV_B):
            s_ref[g, 0:keys, :] = lax.dot_general(kb_ref[0, 0:keys, g * DH_B:(g + 1) * DH_B], qg_ref[g], NT_DIMS,
                                                  preferred_element_type=F32)
        for g in range(KV_B):
            mx = jnp.full((1, nq), NEG_BIG, F32)
            for j in range(n):
                b1 = bias_ref[j * tk:(j + 1) * tk, :]
                blk = s_ref[g, j * tk:(j + 1) * tk, :] * scale2 + jnp.concatenate([b1] * group, axis=1)
                s_ref[g, j * tk:(j + 1) * tk, :] = blk
                mx = jnp.maximum(mx, jnp.max(blk, axis=0, keepdims=True))
            l = jnp.zeros((1, nq), F32)
            for j in range(n):
                p = jnp.exp2(s_ref[g, j * tk:(j + 1) * tk, :] - mx)
                p_ref[g, j * tk:(j + 1) * tk, :] = p.astype(BF16)
                l = l + jnp.sum(p, axis=0, keepdims=True)
            o = jnp.dot(vbt_ref[g, :, 0:keys], p_ref[g, 0:keys, :], preferred_element_type=F32) * (1.0 / l)
            for hh in range(group):
                h = g * group + hh
                o_ref[0, :, h * DH_B:(h + 1) * DH_B] = o[:, hh * tq:(hh + 1) * tq].T.astype(o_ref.dtype)

    for n in range(1, nt + 1):
        pl.when(nkv == n)(functools.partial(attend, n))


def _dsa_prompt(qb, qi, wi_t, kid, kb, vb_t):
    b, s, _ = qb.shape
    tq = min(s, LANES)
    tk = KV_TILE
    nt = s // tk
    topk = min(TOPK, s // 4)
    assert s % tq == 0 and s % tk == 0 and tk >= topk and tk % tq == 0
    assert vb_t.shape == (KV_B, DH_B, b * s) and wi_t.shape == (H_IDX, b * s)
    idx_bits = int(math.ceil(math.log2(s))) + 1
    blk_q = lambda w: pl.BlockSpec((1, tq, w), lambda bb, i: (bb, i, 0))
    blk_s = lambda w: pl.BlockSpec((1, s, w), lambda bb, i: (bb, 0, 0))
    return pl.pallas_call(
        functools.partial(_dsa_prompt_body, tq=tq, tk=tk, nt=nt, topk=topk, idx_bits=idx_bits),
        grid=(b, s // tq),
        in_specs=[blk_q(W_QB), blk_q(W_QI),
                  pl.BlockSpec((H_IDX, tq), lambda bb, i: (0, bb * (s // tq) + i)),
                  blk_s(2 * D_IDX), blk_s(W_KB),
                  pl.BlockSpec((KV_B, DH_B, s), lambda bb, i: (0, 0, bb))],
        out_specs=blk_q(W_QB),
        out_shape=jax.ShapeDtypeStruct((b, s, W_QB), BF16),
        scratch_shapes=[
            pltpu.VMEM((nt, tk, tq), jnp.int32),
            pltpu.VMEM((H_IDX * tq, LANES), BF16),
            pltpu.VMEM((KV_B, (H_B // KV_B) * tq, DH_B), BF16),
            pltpu.VMEM((s, tq), F32),
            pltpu.VMEM((KV_B, s, (H_B // KV_B) * tq), F32),
            pltpu.VMEM((KV_B, s, (H_B // KV_B) * tq), BF16),
        ],
        compiler_params=_cparams(("parallel", "parallel")),
        name="dsa_prompt",
    )(qb, qi, wi_t, kid, kb, vb_t)


def _outproj_body(h_ref, oa_ref, ob_ref, wa_ref, wb_ref, o_ref):
    o_ref[...] = (h_ref[...]
                  + jnp.dot(oa_ref[...], wa_ref[...], preferred_element_type=F32)
                  + jnp.dot(ob_ref[...], wb_ref[...], preferred_element_type=F32))


def _outproj(h, oa, ob, w_a, w_b):
    m, d = h.shape
    tm = min(m, 512)
    assert m % tm == 0
    row = lambda w: pl.BlockSpec((tm, w), lambda i: (i, 0))
    const = lambda a: pl.BlockSpec(a.shape, lambda i: (0, 0), pipeline_mode=pl.Buffered(1))
    return pl.pallas_call(
        _outproj_body,
        grid=(m // tm,),
        in_specs=[row(d), row(oa.shape[1]), row(ob.shape[1]), const(w_a), const(w_b)],
        out_specs=row(d),
        out_shape=jax.ShapeDtypeStruct((m, d), F32),
        compiler_params=_cparams(("parallel",)),
        name="outproj",
    )(h, oa, ob, w_a, w_b)


def _diff_decode_body(pt_ref, q_ref, ks_ref, vs_ref, lam_ref, gh_ref, *rest, pp, lam_init):
    k_refs = rest[:pp]
    v_refs = rest[pp:2 * pp]
    o_ref, m_ref, l_ref, acc_ref = rest[2 * pp:]
    j = pl.program_id(1)
    q = q_ref[0]
    lane = lax.broadcasted_iota(jnp.int32, (1, LANES), 1)
    zero = jnp.zeros_like(q)
    qq = jnp.concatenate([jnp.where(lane < D_A, q, zero), jnp.where(lane >= D_A, q, zero)], axis=0)
    rows_per_page = PAGE_SIZE * H_A

    @pl.when(j == 0)
    def _():
        ks = ks_ref[0].astype(BF16).astype(F32)
        prod = q.astype(F32) * ks
        s1 = jnp.sum(jnp.where(lane < D_A, prod, 0.0), axis=-1, keepdims=True)
        s2 = jnp.sum(jnp.where(lane >= D_A, prod, 0.0), axis=-1, keepdims=True)
        m_ref[...] = jnp.concatenate([s1, s2], axis=0)
        l_ref[...] = jnp.ones_like(l_ref)
        vs = vs_ref[0].astype(BF16).astype(F32)
        acc_ref[...] = jnp.concatenate([vs, vs], axis=0)

    head_of_row = lax.broadcasted_iota(jnp.int32, (2 * H_A, rows_per_page), 0) % H_A
    head_of_lane = lax.broadcasted_iota(jnp.int32, (2 * H_A, rows_per_page), 1) % H_A
    bias = jnp.where(head_of_row == head_of_lane, 0.0, NEG_BIG).astype(F32)
    s_all = []
    for p in range(pp):
        kp = k_refs[p][0].astype(BF16)
        s_all.append(lax.dot_general(qq, kp, NT_DIMS, preferred_element_type=F32) + bias)
    m = m_ref[...]
    m_new = m
    for s in s_all:
        m_new = jnp.maximum(m_new, jnp.max(s, axis=-1, keepdims=True))
    alpha = jnp.exp(m - m_new)
    l = alpha * l_ref[...]
    acc = alpha * acc_ref[...]
    for p in range(pp):
        pr = jnp.exp(s_all[p] - m_new)
        l = l + jnp.sum(pr, axis=-1, keepdims=True)
        acc = acc + jnp.dot(pr.astype(BF16), v_refs[p][0].astype(BF16), preferred_element_type=F32)
    m_ref[...] = m_new
    l_ref[...] = l
    acc_ref[...] = acc

    @pl.when(j == pl.num_programs(1) - 1)
    def _():
        o = acc / l
        lam = _lambda_value(lam_ref, lam_init)
        o = o[:H_A] - lam * o[H_A:]
        o_ref[0] = (_rms(o, gh_ref[...]) * (1.0 - lam_init)).astype(o_ref.dtype)


def _diff_decode(page_table, qa, k_self, v_self, lam_params, g_head, cache_k, cache_v, *, lam_init):
    db, n_pages = page_table.shape
    pp = 8 if n_pages % 8 == 0 else 1
    rows = PAGE_SIZE * H_A
    tok = pl.BlockSpec((1, H_A, DV_A), lambda b, j, pt: (b, 0, 0))
    page = lambda p: pl.BlockSpec((1, rows, DV_A), lambda b, j, pt: (pt[b, j * pp + p], 0, 0))
    grid_spec = pltpu.PrefetchScalarGridSpec(
        num_scalar_prefetch=1,
        grid=(db, n_pages // pp),
        in_specs=[tok, tok, tok,
                  pl.BlockSpec(lam_params.shape, lambda b, j, pt: (0, 0)),
                  pl.BlockSpec(g_head.shape, lambda b, j, pt: (0, 0))]
                 + [page(p) for p in range(pp)] + [page(p) for p in range(pp)],
        out_specs=tok,
        scratch_shapes=[pltpu.VMEM((2 * H_A, 1), F32), pltpu.VMEM((2 * H_A, 1), F32),
                        pltpu.VMEM((2 * H_A, DV_A), F32)],
    )
    return pl.pallas_call(
        functools.partial(_diff_decode_body, pp=pp, lam_init=lam_init),
        grid_spec=grid_spec,
        out_shape=jax.ShapeDtypeStruct((db, H_A, DV_A), BF16),
        compiler_params=_cparams(("parallel", "arbitrary")),
        name="diff_decode",
    )(page_table, qa, k_self, v_self, lam_params, g_head, *([cache_k] * pp), *([cache_v] * pp))


def _idx_decode_body(pt_ref, qs_ref, w_ref, kself_ref, *rest, pp, n_steps):
    k_refs = rest[:pp]
    o_ref = rest[pp]
    j = pl.program_id(1)
    qs = qs_ref[0]
    w = w_ref[0]

    @pl.when(j < n_steps)
    def _():
        for p in range(pp):
            kp = k_refs[p][0].astype(BF16)
            r = jnp.dot(qs, kp, preferred_element_type=F32)
            o_ref[0, p:p + 1, :] = jnp.sum(w * jnp.maximum(r, 0.0), axis=0, keepdims=True)

    @pl.when(j == n_steps)
    def _():
        ks = kself_ref[0].astype(BF16).astype(F32)
        r = jnp.sum(qs.astype(F32) * ks, axis=-1, keepdims=True)
        sc = jnp.sum(w * jnp.maximum(r, 0.0), axis=0, keepdims=True)
        lane = lax.broadcasted_iota(jnp.int32, (1, LANES), 1)
        o_ref[0] = jnp.full((pp, LANES), -jnp.inf, F32)
        o_ref[0, 0:1, :] = jnp.where(lane == 0, sc, -jnp.inf)


def _idx_decode(page_table, qs, w, k_self, cache_idx_t):
    db, n_pages = page_table.shape
    pp = min(IDX_PAGES_PER_STEP, n_pages)
    assert n_pages % pp == 0
    n_steps = n_pages // pp
    page = lambda p: pl.BlockSpec(
        (1, D_IDX, PAGE_SIZE),
        lambda b, j, pt: (pt[b, jnp.minimum(j * pp + p, n_pages - 1)], 0, 0))
    grid_spec = pltpu.PrefetchScalarGridSpec(
        num_scalar_prefetch=1,
        grid=(db, n_steps + 1),
        in_specs=[pl.BlockSpec((1, H_IDX, D_IDX), lambda b, j, pt: (b, 0, 0)),
                  pl.BlockSpec((1, H_IDX, 1), lambda b, j, pt: (b, 0, 0)),
                  pl.BlockSpec((1, 1, D_IDX), lambda b, j, pt: (b, 0, 0))]
                 + [page(p) for p in range(pp)],
        out_specs=pl.BlockSpec((1, pp, LANES), lambda b, j, pt: (b, j, 0)),
    )
    return pl.pallas_call(
        functools.partial(_idx_decode_body, pp=pp, n_steps=n_steps),
        grid_spec=grid_spec,
        out_shape=jax.ShapeDtypeStruct((db, n_pages + pp, LANES), F32),
        compiler_params=_cparams(("parallel", "arbitrary")),
        name="idx_decode",
    )(page_table, qs, w, k_self, *([cache_idx_t] * pp))


def _select_decode_body(s_ref, thr_ref, jb_ref, *, topk, idx_bits):
    keys = _sort_key(s_ref[...])
    n = keys.shape[1]
    col = lax.broadcasted_iota(jnp.int32, keys.shape, 1)

    def row_count(pred):
        hit = jnp.where(pred, 1.0, 0.0)
        part = jnp.zeros((keys.shape[0], LANES), F32)
        for c in range(0, n, LANES):
            part = part + hit[:, c:c + LANES]
        return jnp.sum(part, axis=-1, keepdims=True).astype(jnp.int32)

    like = jnp.zeros((keys.shape[0], 1), jnp.int32)
    thr = _kth_largest_key(lambda c: row_count(keys >= c), like, topk)
    need = topk - row_count(keys > thr)
    jb = _tie_index_bound(lambda c: row_count((keys == thr) & (col < c)), need, like, idx_bits)
    thr_ref[...] = thr
    jb_ref[...] = jb


def _select_decode(scores, topk):
    db, n = scores.shape
    idx_bits = int(math.ceil(math.log2(n))) + 1
    return pl.pallas_call(
        functools.partial(_select_decode_body, topk=topk, idx_bits=idx_bits),
        out_shape=[jax.ShapeDtypeStruct((db, 1), jnp.int32)] * 2,
        compiler_params=pltpu.CompilerParams(vmem_limit_bytes=VMEM_LIMIT),
        name="select_decode",
    )(scores)


def _dsa_decode_body(pt_ref, thr_ref, jb_ref, q_ref, ks_ref, vs_ref, sc_ref, *rest, pp, n_steps):
    k_refs = rest[:pp]
    v_refs = rest[pp:2 * pp]
    o_ref, m_ref, l_ref, acc_ref = rest[2 * pp:]
    b = pl.program_id(0)
    j = pl.program_id(1)
    q = q_ref[0]
    thr = thr_ref[b]
    jb = jb_ref[b]
    scale = DH_B ** -0.5
    group = H_B // KV_B
    rows_per_page = PAGE_SIZE * KV_B

    @pl.when(j == 0)
    def _():
        m_ref[...] = jnp.full_like(m_ref, NEG_BIG)
        l_ref[...] = jnp.zeros_like(l_ref)
        acc_ref[...] = jnp.zeros_like(acc_ref)

    keys = _sort_key(sc_ref[0])
    idx = (lax.broadcasted_iota(jnp.int32, (pp, LANES), 0) + j * pp) * PAGE_SIZE \
        + lax.broadcasted_iota(jnp.int32, (pp, LANES), 1)
    sel = (keys > thr) | ((keys == thr) & (idx < jb))

    @pl.when(j < n_steps)
    def _():
        spread = (lax.broadcasted_iota(jnp.int32, (LANES, rows_per_page), 1) // KV_B
                  == lax.broadcasted_iota(jnp.int32, (LANES, rows_per_page), 0))
        sel2 = jnp.dot(jnp.where(sel, 1.0, 0.0).astype(BF16), jnp.where(spread, 1.0, 0.0).astype(BF16),
                       preferred_element_type=F32) > 0.5
        kv_of_row = lax.broadcasted_iota(jnp.int32, (H_B, rows_per_page), 0) // group
        kv_of_lane = lax.broadcasted_iota(jnp.int32, (H_B, rows_per_page), 1) % KV_B
        same_kv = kv_of_row == kv_of_lane
        s_all, ok_all = [], []
        for p in range(pp):
            kp = k_refs[p][0].astype(BF16)
            s = lax.dot_general(q, kp, NT_DIMS, preferred_element_type=F32) * scale
            ok = same_kv & sel2[p:p + 1, :]
            s_all.append(jnp.where(ok, s, NEG_BIG))
            ok_all.append(ok)
        m = m_ref[...]
        m_new = m
        for s in s_all:
            m_new = jnp.maximum(m_new, jnp.max(s, axis=-1, keepdims=True))
        alpha = jnp.exp(m - m_new)
        l = alpha * l_ref[...]
        acc = alpha * acc_ref[...]
        for p in range(pp):
            pr = jnp.where(ok_all[p], jnp.exp(s_all[p] - m_new), 0.0)
            l = l + jnp.sum(pr, axis=-1, keepdims=True)
            acc = acc + jnp.dot(pr.astype(BF16), v_refs[p][0].astype(BF16), preferred_element_type=F32)
        m_ref[...] = m_new
        l_ref[...] = l
        acc_ref[...] = acc

    @pl.when(j == n_steps)
    def _():
        row8 = lax.broadcasted_iota(jnp.int32, (H_B, LANES), 0)
        ks = ks_ref[0].astype(BF16).astype(F32)
        vs = vs_ref[0].astype(BF16).astype(F32)
        kexp = jnp.where(row8 < group, ks[0:1], ks[1:2])
        vexp = jnp.where(row8 < group, vs[0:1], vs[1:2])
        s = jnp.sum(q.astype(F32) * kexp, axis=-1, keepdims=True) * scale
        ok = sel[0:1, 0:1]
        s = jnp.where(ok, s, NEG_BIG)
        m = m_ref[...]
        m_new = jnp.maximum(m, s)
        alpha = jnp.exp(m - m_new)
        pr = jnp.where(ok, jnp.exp(s - m_new), 0.0)
        l = alpha * l_ref[...] + pr
        acc = alpha * acc_ref[...] + pr * vexp
        o_ref[0] = (acc / l).astype(o_ref.dtype)


def _dsa_decode(page_table, thr, jb, qb, k_self, v_self, scores3, cache_k, cache_v):
    db, n_pages = page_table.shape
    pp = min(DSA_PAGES_PER_STEP, n_pages)
    assert n_pages % pp == 0 and scores3.shape[1] >= n_pages + pp
    n_steps = n_pages // pp
    rows = PAGE_SIZE * KV_B
    page = lambda p: pl.BlockSpec(
        (1, rows, DH_B),
        lambda b, j, pt, t, jj: (pt[b, jnp.minimum(j * pp + p, n_pages - 1)], 0, 0))
    tok = lambda r: pl.BlockSpec((1, r, DH_B), lambda b, j, pt, t, jj: (b, 0, 0))
    grid_spec = pltpu.PrefetchScalarGridSpec(
        num_scalar_prefetch=3,
        grid=(db, n_steps + 1),
        in_specs=[tok(H_B), tok(KV_B), tok(KV_B),
                  pl.BlockSpec((1, pp, LANES), lambda b, j, pt, t, jj: (b, j, 0))]
                 + [page(p) for p in range(pp)] + [page(p) for p in range(pp)],
        out_specs=tok(H_B),
        scratch_shapes=[pltpu.VMEM((H_B, 1), F32), pltpu.VMEM((H_B, 1), F32),
                        pltpu.VMEM((H_B, DH_B), F32)],
    )
    return pl.pallas_call(
        functools.partial(_dsa_decode_body, pp=pp, n_steps=n_steps),
        grid_spec=grid_spec,
        out_shape=jax.ShapeDtypeStruct((db, H_B, DH_B), BF16),
        compiler_params=_cparams(("parallel", "arbitrary")),
        name="dsa_decode",
    )(page_table, thr, jb, qb, k_self, v_self, scores3, *([cache_k] * pp), *([cache_v] * pp))


def _rope_inv(half):
    return ROPE_THETA ** (-jnp.arange(half, dtype=F32) / half)


def kernel(x_prompt, x_sample, cache_diff_k, cache_diff_v, cache_dsa_k, cache_dsa_v, cache_idx_k,
           page_table, norm_ffn1, w_ffn1_in, w_ffn1_out, norm_mix, w_in, lambda_q1, lambda_k1,
           lambda_q2, lambda_k2, norm_head, w_out, norm_ffn2, w_ffn2_in, w_ffn2_out, norm_final):
    depth = w_in.shape[0]
    assert depth == 1 and x_sample.shape[1] == 1
    bsz, seq, d = x_prompt.shape
    db = x_sample.shape[0]
    n_phys = cache_diff_k.shape[1]
    n_pages = page_table.shape[1]
    past = n_pages * PAGE_SIZE
    lam_init = 0.8 - 0.6 * math.exp(-0.3 * 0)

    w1i, w1o = w_ffn1_in[0].astype(BF16), w_ffn1_out[0].astype(BF16)
    w2i, w2o = w_ffn2_in[0].astype(BF16), w_ffn2_out[0].astype(BF16)
    wi_all = w_in[0].astype(BF16)
    w_main = wi_all[:, :W_MAIN]
    w_k = wi_all[:, W_MAIN:W_MAIN + D_IDX]
    w_kidx2 = jnp.concatenate([w_k, w_k], axis=1)
    w_widx = jnp.pad(wi_all[:, W_MAIN + D_IDX:], ((0, 0), (0, LANES - H_IDX)))
    wo_all = w_out[0].astype(BF16)
    wo_a, wo_b = wo_all[:H_A * DV_A], wo_all[H_A * DV_A:]
    g1, gm, g2 = norm_ffn1[0][None], norm_mix[0][None], norm_ffn2[0][None]
    gh, gf = norm_head[0][None], norm_final[None]
    lam_params = jnp.stack([lambda_q1[0], lambda_k1[0], lambda_q2[0], lambda_k2[0]]).astype(F32)
    inv32, inv64h = _rope_inv(D_A // 2), _rope_inv(DH_B // 2)
    inv64 = jnp.tile(inv32, LANES // (D_A // 2))[None]
    inv128 = jnp.tile(inv64h, LANES // (DH_B // 2))[None]

    def front(x2d, pos_offset, period, prompt):
        h = _ffn(x2d, g1, w1i, w1o, gf, final_norm=False)
        return h, _inproj(h, gm, w_main, w_kidx2, w_widx, inv64, inv128,
                          pos_offset=pos_offset, period=period, prompt=prompt)

    def back(h, oa, ob):
        h = _outproj(h, oa, ob, wo_a, wo_b)
        return _ffn(h, g2, w2i, w2o, gf, final_norm=True)

    mp = bsz * seq
    hp, (qa, ka, kab, va, vat, qb, kb, kbb, vb, vbt, qi, ki, kid, wit) = front(
        x_prompt.reshape(mp, d), 0, seq, True)
    r3 = lambda a: a.reshape(bsz, seq, a.shape[-1])
    oa = _diff_prompt(r3(qa), r3(kab), vat, lam_params, gh.T, lam_init=lam_init)
    ob = _dsa_prompt(r3(qb), r3(qi), wit, r3(kid), r3(kbb), vbt)
    y_prompt = back(hp, oa.reshape(mp, -1), ob.reshape(mp, -1)).reshape(bsz, seq, d)

    hs, (qa_s, ka_s, va_s, qb_s, kb_s, vb_s, qi_s, ki_s, wi_s) = front(
        x_sample.reshape(db, d), past, 1, False)
    oa_s = _diff_decode(
        page_table, qa_s.reshape(db, H_A, DV_A), ka_s.reshape(db, H_A, DV_A),
        va_s.reshape(db, H_A, DV_A), lam_params, gh,
        cache_diff_k.reshape(n_phys, PAGE_SIZE * H_A, DV_A),
        cache_diff_v.reshape(n_phys, PAGE_SIZE * H_A, DV_A), lam_init=lam_init)
    scores3 = _idx_decode(page_table, qi_s.reshape(db, H_IDX, D_IDX), wi_s.reshape(db, H_IDX, 1),
                          ki_s.reshape(db, 1, D_IDX),
                          jnp.swapaxes(cache_idx_k.reshape(n_phys, PAGE_SIZE, D_IDX), 1, 2))
    thr, jb = _select_decode(scores3.reshape(db, -1), min(TOPK, (past + 1) // 4))
    ob_s = _dsa_decode(
        page_table, thr.reshape(db), jb.reshape(db), qb_s.reshape(db, H_B, DH_B),
        kb_s.reshape(db, KV_B, DH_B), vb_s.reshape(db, KV_B, DH_B), scores3,
        cache_dsa_k.reshape(n_phys, PAGE_SIZE * KV_B, DH_B),
        cache_dsa_v.reshape(n_phys, PAGE_SIZE * KV_B, DH_B))
    y_sample = back(hs, oa_s.reshape(db, -1), ob_s.reshape(db, -1)).reshape(db, 1, d)

    p5 = lambda a, h, w: a.reshape(1, bsz, seq, h, w)
    s5 = lambda a, h, w: a.reshape(1, db, 1, h, w)
    return (y_prompt, y_sample,
            p5(ka, H_A, 2 * D_A), p5(va, H_A, DV_A), p5(kb, KV_B, DH_B), p5(vb, KV_B, DH_B),
            ki.reshape(1, bsz, seq, D_IDX),
            s5(ka_s, H_A, 2 * D_A), s5(va_s, H_A, DV_A), s5(kb_s, KV_B, DH_B), s5(vb_s, KV_B, DH_B),
            ki_s.reshape(1, db, 1, D_IDX))
```

```python
import functools
import math

import jax
import jax.numpy as jnp
from jax import lax
from jax.experimental import pallas as pl
from jax.experimental.pallas import tpu as pltpu

H_A = 8
D_A = 64
DV_A = 2 * D_A
H_B = 8
KV_B = 2
DH_B = 128
H_IDX = 16
D_IDX = 64
TOPK = 256
ROPE_THETA = 10000.0
EPS = 1e-6
PAGE_SIZE = 128

LANES = 128
FFN_ROWS = 1024
DIFF_HEADS_PER_STEP = 4
DSA_QUERY_TILE = 256
KV_TILE = 256
LOG2E = 1.4426950408889634
IDX_PAGES_PER_STEP = 32
DIFF_PAGES_PER_STEP = 16
DSA_PAGES_PER_STEP = 16
NEG_BIG = -1e30
INT_MIN = -(2 ** 31)
VMEM_LIMIT = 56 * 1024 * 1024

W_QA = H_A * 2 * D_A
W_KA = H_A * 2 * D_A
W_VA = H_A * DV_A
W_QB = H_B * DH_B
W_KB = KV_B * DH_B
W_VB = KV_B * DH_B
W_QI = H_IDX * D_IDX
W_MAIN = W_QA + W_KA + W_VA + W_QB + W_KB + W_VB + W_QI

F32 = jnp.float32
BF16 = jnp.bfloat16
NT_DIMS = (((1,), (1,)), ((), ()))


def _cparams(sem):
    return pltpu.CompilerParams(dimension_semantics=sem, vmem_limit_bytes=VMEM_LIMIT)


def _rms(x, g):
    return x * lax.rsqrt(jnp.mean(x * x, axis=-1, keepdims=True) + EPS) * g


def _sort_key(x):
    b = lax.bitcast_convert_type(x, jnp.int32)
    return b ^ ((b >> 31) & jnp.int32(0x7FFFFFFF))


def _kth_largest_key(count_ge, like, k):
    def body(it, t):
        c = t | lax.shift_left(jnp.int32(1), 31 - it)
        cnt = count_ge(c ^ jnp.int32(INT_MIN))
        return jnp.where(cnt >= k, c, t)

    t = lax.fori_loop(0, 32, body, jnp.zeros_like(like))
    return t ^ jnp.int32(INT_MIN)


def _tie_index_bound(count_eq_below, need, like, nbits):
    def body(it, jb):
        c = jb | lax.shift_left(jnp.int32(1), nbits - 1 - it)
        return jnp.where(count_eq_below(c) <= need, c, jb)

    return lax.fori_loop(0, nbits, body, jnp.zeros_like(like))


def _ffn_body(x_ref, g_ref, wg_ref, wu_ref, wo_ref, gf_ref, o_ref, xn_ref, *, final_norm):
    j = pl.program_id(1)

    @pl.when(j == 0)
    def _():
        x = x_ref[...]
        xn_ref[...] = _rms(x, g_ref[...]).astype(BF16)
        o_ref[...] = x

    xn = xn_ref[...]
    gate = jnp.dot(xn, wg_ref[...], preferred_element_type=F32)
    up = jnp.dot(xn, wu_ref[...], preferred_element_type=F32)
    act = (gate * jax.nn.sigmoid(gate) * (0.5 * up)).astype(BF16)
    o_ref[...] += jnp.dot(act, wo_ref[...], preferred_element_type=F32)

    if final_norm:
        @pl.when(j == pl.num_programs(1) - 1)
        def _():
            o_ref[...] = _rms(o_ref[...], gf_ref[...])


def _ffn(x, g, w_in, w_out, g_final, *, final_norm):
    m, d = x.shape
    f = w_out.shape[0]
    tm = min(m, FFN_ROWS)
    tf = min(f, 512)
    assert m % tm == 0 and f % tf == 0
    nf = f // tf
    return pl.pallas_call(
        functools.partial(_ffn_body, final_norm=final_norm),
        grid=(m // tm, nf),
        in_specs=[
            pl.BlockSpec((tm, d), lambda i, j: (i, 0)),
            pl.BlockSpec((1, d), lambda i, j: (0, 0)),
            pl.BlockSpec((d, tf), lambda i, j: (0, j)),
            pl.BlockSpec((d, tf), lambda i, j: (0, j + nf)),
            pl.BlockSpec((tf, d), lambda i, j: (j, 0)),
            pl.BlockSpec((1, d), lambda i, j: (0, 0)),
        ],
        out_specs=pl.BlockSpec((tm, d), lambda i, j: (i, 0)),
        out_shape=jax.ShapeDtypeStruct((m, d), F32),
        scratch_shapes=[pltpu.VMEM((tm, d), BF16)],
        compiler_params=_cparams(("parallel", "arbitrary")),
        name="ffn_final" if final_norm else "ffn",
    )(x, g, w_in, w_in, w_out, g_final)


def _inproj_body(x_ref, g_ref, wm_ref, wk_ref, ww_ref, inv64_ref, inv128_ref, *out_refs,
                 tm, pos_offset, period, prompt):
    if prompt:
        (qa_ref, ka_ref, kab_ref, va_ref, vat_ref, qb_ref, kb_ref, kbb_ref, vb_ref, vbt_ref,
         qi_ref, ki_ref, kid_ref, wi_ref) = out_refs
    else:
        qa_ref, ka_ref, va_ref, qb_ref, kb_ref, vb_ref, qi_ref, ki_ref, wi_ref = out_refs
    xn = _rms(x_ref[...], g_ref[...]).astype(BF16)
    if period == 1:
        pos = jnp.full((tm, 1), float(pos_offset), F32)
    else:
        row0 = (pl.program_id(0) * tm) % period
        pos = (pos_offset + row0 + lax.broadcasted_iota(jnp.int32, (tm, 1), 0)).astype(F32)
    lane = lax.broadcasted_iota(jnp.int32, (1, LANES), 1)
    lo64 = (lane % 64) < 32
    lo128 = lane < 64
    ang64 = pos * inv64_ref[...]
    ang128 = pos * inv128_ref[...]
    cos64 = jnp.cos(ang64)
    sin64 = jnp.where(lo64, -jnp.sin(ang64), jnp.sin(ang64))
    cos128 = jnp.cos(ang128)
    sin128 = jnp.where(lo128, -jnp.sin(ang128), jnp.sin(ang128))

    def rope64(z):
        rot = jnp.where(lo64, pltpu.roll(z, 96, 1), pltpu.roll(z, 32, 1))
        return z * cos64 + rot * sin64

    def rope128(z):
        return z * cos128 + pltpu.roll(z, 64, 1) * sin128

    def rows(ref):
        def put(col, zc):
            ref[:, col:col + LANES] = zc.astype(ref.dtype)
        return put

    def tiles(ref):
        def put(col, zc):
            ref[col // LANES] = zc.T.astype(ref.dtype)
        return put

    def region(start, width, fn, sinks):
        step = 512 if width % 512 == 0 else width
        for c0 in range(0, width, step):
            z = jnp.dot(xn, wm_ref[:, start + c0:start + c0 + step], preferred_element_type=F32)
            for c in range(0, step, LANES):
                zc = fn(z[:, c:c + LANES])
                for put in sinks:
                    put(c0 + c, zc)

    ident = lambda z: z
    off = 0
    region(off, W_QA, lambda z: rope64(z) * (D_A ** -0.5), [rows(qa_ref)]); off += W_QA
    region(off, W_KA, rope64, [rows(ka_ref)] + ([rows(kab_ref)] if prompt else [])); off += W_KA
    region(off, W_VA, ident, [rows(va_ref)] + ([tiles(vat_ref)] if prompt else [])); off += W_VA
    region(off, W_QB, rope128, [rows(qb_ref)]); off += W_QB
    region(off, W_KB, rope128, [rows(kb_ref)] + ([rows(kbb_ref)] if prompt else [])); off += W_KB
    region(off, W_VB, ident, [rows(vb_ref)] + ([tiles(vbt_ref)] if prompt else [])); off += W_VB
    region(off, W_QI, lambda z: rope64(z) * (D_IDX ** -0.5), [rows(qi_ref)]); off += W_QI
    kd = rope64(jnp.dot(xn, wk_ref[...], preferred_element_type=F32))
    ki_ref[...] = kd[:, :D_IDX]
    wz = jnp.dot(xn, ww_ref[...], preferred_element_type=F32) * (H_IDX ** -0.5)
    if prompt:
        kid_ref[...] = kd.astype(BF16)
        wi_ref[...] = wz.T[:H_IDX]
    else:
        wi_ref[...] = wz[:, :H_IDX]


def _inproj(x, g, w_main, w_kidx2, w_widx, inv64, inv128, *, pos_offset, period, prompt):
    m, d = x.shape
    tm = min(m, KV_TILE)
    assert m % tm == 0 and (period == 1 or period % tm == 0)
    nt = m // tm
    row = lambda w, dt: (jax.ShapeDtypeStruct((m, w), dt), pl.BlockSpec((tm, w), lambda i: (i, 0)))
    tile = lambda h: (jax.ShapeDtypeStruct((h, LANES, m), BF16),
                      pl.BlockSpec((h, LANES, tm), lambda i: (0, 0, i)))
    const = lambda a: pl.BlockSpec(a.shape, lambda i: (0, 0), pipeline_mode=pl.Buffered(1))
    if prompt:
        outs = [row(W_QA, BF16), row(W_KA, F32), row(W_KA, BF16), row(W_VA, F32), tile(H_A),
                row(W_QB, BF16), row(W_KB, F32), row(W_KB, BF16), row(W_VB, F32), tile(KV_B),
                row(W_QI, BF16), row(D_IDX, F32), row(2 * D_IDX, BF16),
                (jax.ShapeDtypeStruct((H_IDX, m), F32), pl.BlockSpec((H_IDX, tm), lambda i: (0, i)))]
    else:
        outs = [row(W_QA, BF16), row(W_KA, F32), row(W_VA, F32), row(W_QB, BF16), row(W_KB, F32),
                row(W_VB, F32), row(W_QI, BF16), row(D_IDX, F32), row(H_IDX, F32)]
    return pl.pallas_call(
        functools.partial(_inproj_body, tm=tm, pos_offset=pos_offset, period=period, prompt=prompt),
        grid=(nt,),
        in_specs=[pl.BlockSpec((tm, d), lambda i: (i, 0)), const(g), const(w_main), const(w_kidx2),
                  const(w_widx), const(inv64), const(inv128)],
        out_specs=[o[1] for o in outs],
        out_shape=[o[0] for o in outs],
        compiler_params=_cparams(("parallel",)),
        name="inproj",
    )(x, g, w_main, w_kidx2, w_widx, inv64, inv128)


def _lambda_value(lam_ref, lam_init):
    lp = lam_ref[...]
    a = jnp.sum(lp[0:1] * lp[1:2], axis=-1, keepdims=True)
    b = jnp.sum(lp[2:3] * lp[3:4], axis=-1, keepdims=True)
    return jnp.exp(a) - jnp.exp(b) + lam_init


def _diff_prompt_body(q_ref, k_ref, vt_ref, lam_ref, ght_ref, o_ref, s_ref, p_ref, *, tq, nt, nh, lam_init):
    i = pl.program_id(2)
    lane = lax.broadcasted_iota(jnp.int32, (1, LANES), 1)
    nq = 2 * tq
    kr = lax.broadcasted_iota(jnp.int32, (tq, tq), 0)
    qc = lax.broadcasted_iota(jnp.int32, (tq, tq), 1)
    bias = jnp.where(kr <= qc, 0.0, NEG_BIG).astype(F32)
    bias = jnp.concatenate([bias, bias], axis=1)
    lam = _lambda_value(lam_ref, lam_init)

    def attend(n):
        keys = n * tq
        cols = [slice(hh * DV_A, (hh + 1) * DV_A) for hh in range(nh)]
        for hh in range(nh):
            q = q_ref[0, :, cols[hh]]
            zero = jnp.zeros_like(q)
            qq = jnp.concatenate([jnp.where(lane < D_A, q, zero), jnp.where(lane >= D_A, q, zero)], axis=0)
            s_ref[hh, 0:keys, :] = lax.dot_general(k_ref[0, 0:keys, cols[hh]], qq, NT_DIMS,
                                                   preferred_element_type=F32)
        for hh in range(nh):
            mx = jnp.full((1, nq), NEG_BIG, F32)
            for j in range(n):
                blk = s_ref[hh, j * tq:(j + 1) * tq, :]
                if j == n - 1:
                    blk = blk + bias
                    s_ref[hh, j * tq:(j + 1) * tq, :] = blk
                mx = jnp.maximum(mx, jnp.max(blk, axis=0, keepdims=True))
            l = jnp.zeros((1, nq), F32)
            for j in range(n):
                p = jnp.exp(s_ref[hh, j * tq:(j + 1) * tq, :] - mx)
                p_ref[hh, j * tq:(j + 1) * tq, :] = p.astype(BF16)
                l = l + jnp.sum(p, axis=0, keepdims=True)
            acc = jnp.dot(vt_ref[hh, :, 0:keys], p_ref[hh, 0:keys, :], preferred_element_type=F32)
            o = acc * (1.0 / l)
            o = o[:, :tq] - lam * o[:, tq:]
            o = o * lax.rsqrt(jnp.mean(o * o, axis=0, keepdims=True) + EPS) * (ght_ref[...] * (1.0 - lam_init))
            o_ref[0, :, cols[hh]] = o.T.astype(o_ref.dtype)

    for n in range(1, nt + 1):
        pl.when(i == n - 1)(functools.partial(attend, n))


def _diff_prompt(qa, ka, va_t, lam_params, g_head_t, *, lam_init):
    b, s, _ = qa.shape
    tq = KV_TILE
    nt = s // tq
    nh = DIFF_HEADS_PER_STEP
    assert s % tq == 0 and va_t.shape == (H_A, DV_A, b * s) and H_A % nh == 0
    w = nh * DV_A
    return pl.pallas_call(
        functools.partial(_diff_prompt_body, tq=tq, nt=nt, nh=nh, lam_init=lam_init),
        grid=(b, H_A // nh, nt),
        in_specs=[
            pl.BlockSpec((1, tq, w), lambda bb, h, i: (bb, i, h)),
            pl.BlockSpec((1, s, w), lambda bb, h, i: (bb, 0, h)),
            pl.BlockSpec((nh, DV_A, s), lambda bb, h, i: (h, 0, bb)),
            pl.BlockSpec(lam_params.shape, lambda bb, h, i: (0, 0)),
            pl.BlockSpec(g_head_t.shape, lambda bb, h, i: (0, 0)),
        ],
        out_specs=pl.BlockSpec((1, tq, w), lambda bb, h, i: (bb, i, h)),
        out_shape=jax.ShapeDtypeStruct((b, s, H_A * DV_A), BF16),
        scratch_shapes=[pltpu.VMEM((nh, s, 2 * tq), F32), pltpu.VMEM((nh, s, 2 * tq), BF16)],
        compiler_params=_cparams(("parallel", "parallel", "parallel")),
        name="diff_prompt",
    )(qa, ka, va_t, lam_params, g_head_t)


def _dsa_prompt_body(qb_ref, qi_ref, wit_ref, kd_ref, kb_ref, vbt_ref, o_ref,
                     key_ref, qs_ref, qg_ref, bias_ref, s_ref, p_ref, *, tq, tk, nt, topk, idx_bits):
    i = pl.program_id(1)
    nkv = (i * tq + tq - 1) // tk + 1
    lane = lax.broadcasted_iota(jnp.int32, (1, LANES), 1)
    krow = lax.broadcasted_iota(jnp.int32, (tk, tq), 0)
    qcol = lax.broadcasted_iota(jnp.int32, (tk, tq), 1) + i * tq

    for h in range(H_IDX):
        blk = qi_ref[0, :, (h // 2) * LANES:(h // 2 + 1) * LANES]
        keep = (lane >= D_IDX) if h % 2 else (lane < D_IDX)
        qs_ref[h * tq:(h + 1) * tq, :] = jnp.where(keep, blk, jnp.zeros_like(blk))
    wt = wit_ref[...]

    def score_block(j, _):
        start = pl.multiple_of(j * tk, tk)
        r = lax.dot_general(kd_ref[0, pl.ds(start, tk), :], qs_ref[...], NT_DIMS,
                            preferred_element_type=F32)
        acc = jnp.zeros((tk, tq), F32)
        for h in range(H_IDX):
            acc = acc + wt[h:h + 1, :] * jnp.maximum(r[:, h * tq:(h + 1) * tq], 0.0)
        acc = jnp.where(krow + j * tk <= qcol, acc, -jnp.inf)
        key_ref[j] = _sort_key(acc)
        return 0

    lax.fori_loop(0, nkv, score_block, 0)

    def count(pred_fn):
        def blk(j, part):
            hit = jnp.where(pred_fn(key_ref[j], j), 1.0, 0.0)
            parts = [hit[r:r + 8] for r in range(0, tk, 8)]
            while len(parts) > 1:
                parts = [parts[a] + parts[a + 1] for a in range(0, len(parts), 2)]
            return part + parts[0]
        part = lax.fori_loop(0, nkv, blk, jnp.zeros((8, tq), F32))
        return jnp.sum(part, axis=0, keepdims=True).astype(jnp.int32)

    like = jnp.zeros((1, tq), jnp.int32)
    thr = _kth_largest_key(lambda c: count(lambda kk, j: kk >= c), like, topk)
    need = topk - count(lambda kk, j: kk > thr)
    n_eq = count(lambda kk, j: kk == thr)
    has_ties = jnp.max(jnp.where(n_eq != need, 1.0, 0.0)) > 0.5
    jbound = lax.cond(
        has_ties,
        lambda: _tie_index_bound(
            lambda c: count(lambda kk, j: (kk == thr) & (krow + j * tk < c)), need, like, idx_bits),
        lambda: jnp.full((1, tq), 2 ** idx_bits, jnp.int32))

    scale2 = (DH_B ** -0.5) * LOG2E
    group = H_B // KV_B
    for g in range(KV_B):
        for hh in range(group):
            qg_ref[g, hh * tq:(hh + 1) * tq, :] = qb_ref[0, :, (g * group + hh) * DH_B:(g * group + hh + 1) * DH_B]

    def attend(n):
        keys = n * tk
        for j in range(n):
            kk = key_ref[j]
            kidx = krow + j * tk
            sel = ((kk > thr) | ((kk == thr) & (kidx < jbound))) & (kidx <= qcol)
            bias_ref[j * tk:(j + 1) * tk, :] = jnp.where(sel, 0.0, NEG_BIG).astype(F32)
        nq = group * tq
        for g in range(KV_B):
            s_ref[g, 0:keys, :] = lax.dot_general(kb_ref[0, 0:keys, g * DH_B:(g + 1) * DH_B], qg_ref[g], NT_DIMS,
                                                  preferred_element_type=F32)
        for g in range(KV_B):
            mx = jnp.full((1, nq), NEG_BIG, F32)
            for j in range(n):
                b1 = bias_ref[j * tk:(j + 1) * tk, :]
                blk = s_ref[g, j * tk:(j + 1) * tk, :] * scale2 + jnp.concatenate([b1] * group, axis=1)
                s_ref[g, j * tk:(j + 1) * tk, :] = blk
                mx = jnp.maximum(mx, jnp.max(blk, axis=0, keepdims=True))
            l = jnp.zeros((1, nq), F32)
            for j in range(n):
                p = jnp.exp2(s_ref[g, j * tk:(j + 1) * tk, :] - mx)
                p_ref[g, j * tk:(j + 1) * tk, :] = p.astype(BF16)
                l = l + jnp.sum(p, axis=0, keepdims=True)
            o = jnp.dot(vbt_ref[g, :, 0:keys], p_ref[g, 0:keys, :], preferred_element_type=F32) * (1.0 / l)
            for hh in range(group):
                h = g * group + hh
                o_ref[0, :, h * DH_B:(h + 1) * DH_B] = o[:, hh * tq:(hh + 1) * tq].T.astype(o_ref.dtype)

    for n in range(1, nt + 1):
        pl.when(nkv == n)(functools.partial(attend, n))


def _dsa_prompt(qb, qi, wi_t, kid, kb, vb_t):
    b, s, _ = qb.shape
    tq = min(s, DSA_QUERY_TILE)
    tk = KV_TILE
    nt = s // tk
    topk = min(TOPK, s // 4)
    assert s % tq == 0 and s % tk == 0 and tk >= topk and tk % tq == 0
    assert vb_t.shape == (KV_B, DH_B, b * s) and wi_t.shape == (H_IDX, b * s)
    idx_bits = int(math.ceil(math.log2(s))) + 1
    blk_q = lambda w: pl.BlockSpec((1, tq, w), lambda bb, i: (bb, i, 0))
    blk_s = lambda w: pl.BlockSpec((1, s, w), lambda bb, i: (bb, 0, 0))
    return pl.pallas_call(
        functools.partial(_dsa_prompt_body, tq=tq, tk=tk, nt=nt, topk=topk, idx_bits=idx_bits),
        grid=(b, s // tq),
        in_specs=[blk_q(W_QB), blk_q(W_QI),
                  pl.BlockSpec((H_IDX, tq), lambda bb, i: (0, bb * (s // tq) + i)),
                  blk_s(2 * D_IDX), blk_s(W_KB),
                  pl.BlockSpec((KV_B, DH_B, s), lambda bb, i: (0, 0, bb))],
        out_specs=blk_q(W_QB),
        out_shape=jax.ShapeDtypeStruct((b, s, W_QB), BF16),
        scratch_shapes=[
            pltpu.VMEM((nt, tk, tq), jnp.int32),
            pltpu.VMEM((H_IDX * tq, LANES), BF16),
            pltpu.VMEM((KV_B, (H_B // KV_B) * tq, DH_B), BF16),
            pltpu.VMEM((s, tq), F32),
            pltpu.VMEM((KV_B, s, (H_B // KV_B) * tq), F32),
            pltpu.VMEM((KV_B, s, (H_B // KV_B) * tq), BF16),
        ],
        compiler_params=_cparams(("parallel", "parallel")),
        name="dsa_prompt",
    )(qb, qi, wi_t, kid, kb, vb_t)


def _outproj_body(h_ref, oa_ref, ob_ref, wa_ref, wb_ref, o_ref):
    o_ref[...] = (h_ref[...]
                  + jnp.dot(oa_ref[...], wa_ref[...], preferred_element_type=F32)
                  + jnp.dot(ob_ref[...], wb_ref[...], preferred_element_type=F32))


def _outproj(h, oa, ob, w_a, w_b):
    m, d = h.shape
    tm = min(m, 512)
    assert m % tm == 0
    row = lambda w: pl.BlockSpec((tm, w), lambda i: (i, 0))
    const = lambda a: pl.BlockSpec(a.shape, lambda i: (0, 0), pipeline_mode=pl.Buffered(1))
    return pl.pallas_call(
        _outproj_body,
        grid=(m // tm,),
        in_specs=[row(d), row(oa.shape[1]), row(ob.shape[1]), const(w_a), const(w_b)],
        out_specs=row(d),
        out_shape=jax.ShapeDtypeStruct((m, d), F32),
        compiler_params=_cparams(("parallel",)),
        name="outproj",
    )(h, oa, ob, w_a, w_b)


def _diff_decode_body(pt_ref, q_ref, ks_ref, vs_ref, lam_ref, gh_ref, *rest, pp, lam_init):
    k_refs = rest[:pp]
    v_refs = rest[pp:2 * pp]
    o_ref, m_ref, l_ref, acc_ref = rest[2 * pp:]
    j = pl.program_id(1)
    q = q_ref[0]
    lane = lax.broadcasted_iota(jnp.int32, (1, LANES), 1)
    zero = jnp.zeros_like(q)
    qq = jnp.concatenate([jnp.where(lane < D_A, q, zero), jnp.where(lane >= D_A, q, zero)], axis=0)
    rows_per_page = PAGE_SIZE * H_A

    @pl.when(j == 0)
    def _():
        ks = ks_ref[0].astype(BF16).astype(F32)
        prod = q.astype(F32) * ks
        s1 = jnp.sum(jnp.where(lane < D_A, prod, 0.0), axis=-1, keepdims=True)
        s2 = jnp.sum(jnp.where(lane >= D_A, prod, 0.0), axis=-1, keepdims=True)
        m_ref[...] = jnp.concatenate([s1, s2], axis=0)
        l_ref[...] = jnp.ones_like(l_ref)
        vs = vs_ref[0].astype(BF16).astype(F32)
        acc_ref[...] = jnp.concatenate([vs, vs], axis=0)

    head_of_row = lax.broadcasted_iota(jnp.int32, (2 * H_A, rows_per_page), 0) % H_A
    head_of_lane = lax.broadcasted_iota(jnp.int32, (2 * H_A, rows_per_page), 1) % H_A
    bias = jnp.where(head_of_row == head_of_lane, 0.0, NEG_BIG).astype(F32)
    s_all = []
    for p in range(pp):
        kp = k_refs[p][0].astype(BF16)
        s_all.append(lax.dot_general(qq, kp, NT_DIMS, preferred_element_type=F32) + bias)
    m = m_ref[...]
    m_new = m
    for s in s_all:
        m_new = jnp.maximum(m_new, jnp.max(s, axis=-1, keepdims=True))
    alpha = jnp.exp(m - m_new)
    l = alpha * l_ref[...]
    acc = alpha * acc_ref[...]
    for p in range(pp):
        pr = jnp.exp(s_all[p] - m_new)
        l = l + jnp.sum(pr, axis=-1, keepdims=True)
        acc = acc + jnp.dot(pr.astype(BF16), v_refs[p][0].astype(BF16), preferred_element_type=F32)
    m_ref[...] = m_new
    l_ref[...] = l
    acc_ref[...] = acc

    @pl.when(j == pl.num_programs(1) - 1)
    def _():
        o = acc / l
        lam = _lambda_value(lam_ref, lam_init)
        o = o[:H_A] - lam * o[H_A:]
        o_ref[0] = (_rms(o, gh_ref[...]) * (1.0 - lam_init)).astype(o_ref.dtype)


def _diff_decode(page_table, qa, k_self, v_self, lam_params, g_head, cache_k, cache_v, *, lam_init):
    db, n_pages = page_table.shape
    pp = min(DIFF_PAGES_PER_STEP, n_pages)
    assert n_pages % pp == 0
    rows = PAGE_SIZE * H_A
    tok = pl.BlockSpec((1, H_A, DV_A), lambda b, j, pt: (b, 0, 0))
    page = lambda p: pl.BlockSpec((1, rows, DV_A), lambda b, j, pt: (pt[b, j * pp + p], 0, 0))
    grid_spec = pltpu.PrefetchScalarGridSpec(
        num_scalar_prefetch=1,
        grid=(db, n_pages // pp),
        in_specs=[tok, tok, tok,
                  pl.BlockSpec(lam_params.shape, lambda b, j, pt: (0, 0)),
                  pl.BlockSpec(g_head.shape, lambda b, j, pt: (0, 0))]
                 + [page(p) for p in range(pp)] + [page(p) for p in range(pp)],
        out_specs=tok,
        scratch_shapes=[pltpu.VMEM((2 * H_A, 1), F32), pltpu.VMEM((2 * H_A, 1), F32),
                        pltpu.VMEM((2 * H_A, DV_A), F32)],
    )
    return pl.pallas_call(
        functools.partial(_diff_decode_body, pp=pp, lam_init=lam_init),
        grid_spec=grid_spec,
        out_shape=jax.ShapeDtypeStruct((db, H_A, DV_A), BF16),
        compiler_params=_cparams(("parallel", "arbitrary")),
        name="diff_decode",
    )(page_table, qa, k_self, v_self, lam_params, g_head, *([cache_k] * pp), *([cache_v] * pp))


def _idx_decode_body(pt_ref, qs_ref, w_ref, kself_ref, *rest, pp, n_steps):
    k_refs = rest[:pp]
    o_ref = rest[pp]
    j = pl.program_id(1)
    qs = qs_ref[0]
    w = w_ref[0]

    @pl.when(j < n_steps)
    def _():
        for p in range(pp):
            kp = k_refs[p][0].astype(BF16)
            r = jnp.dot(qs, kp, preferred_element_type=F32)
            o_ref[0, p:p + 1, :] = jnp.sum(w * jnp.maximum(r, 0.0), axis=0, keepdims=True)

    @pl.when(j == n_steps)
    def _():
        ks = kself_ref[0].astype(BF16).astype(F32)
        r = jnp.sum(qs.astype(F32) * ks, axis=-1, keepdims=True)
        sc = jnp.sum(w * jnp.maximum(r, 0.0), axis=0, keepdims=True)
        lane = lax.broadcasted_iota(jnp.int32, (1, LANES), 1)
        o_ref[0] = jnp.full((pp, LANES), -jnp.inf, F32)
        o_ref[0, 0:1, :] = jnp.where(lane == 0, sc, -jnp.inf)


def _idx_decode(page_table, qs, w, k_self, cache_idx_t):
    db, n_pages = page_table.shape
    pp = min(IDX_PAGES_PER_STEP, n_pages)
    assert n_pages % pp == 0
    n_steps = n_pages // pp
    page = lambda p: pl.BlockSpec(
        (1, D_IDX, PAGE_SIZE),
        lambda b, j, pt: (pt[b, jnp.minimum(j * pp + p, n_pages - 1)], 0, 0))
    grid_spec = pltpu.PrefetchScalarGridSpec(
        num_scalar_prefetch=1,
        grid=(db, n_steps + 1),
        in_specs=[pl.BlockSpec((1, H_IDX, D_IDX), lambda b, j, pt: (b, 0, 0)),
                  pl.BlockSpec((1, H_IDX, 1), lambda b, j, pt: (b, 0, 0)),
                  pl.BlockSpec((1, 1, D_IDX), lambda b, j, pt: (b, 0, 0))]
                 + [page(p) for p in range(pp)],
        out_specs=pl.BlockSpec((1, pp, LANES), lambda b, j, pt: (b, j, 0)),
    )
    return pl.pallas_call(
        functools.partial(_idx_decode_body, pp=pp, n_steps=n_steps),
        grid_spec=grid_spec,
        out_shape=jax.ShapeDtypeStruct((db, n_pages + pp, LANES), F32),
        compiler_params=_cparams(("parallel", "arbitrary")),
        name="idx_decode",
    )(page_table, qs, w, k_self, *([cache_idx_t] * pp))


def _select_decode_body(s_ref, thr_ref, jb_ref, *, topk, idx_bits):
    keys = _sort_key(s_ref[...])
    n = keys.shape[1]
    col = lax.broadcasted_iota(jnp.int32, keys.shape, 1)

    def row_count(pred):
        hit = jnp.where(pred, 1.0, 0.0)
        part = jnp.zeros((keys.shape[0], LANES), F32)
        for c in range(0, n, LANES):
            part = part + hit[:, c:c + LANES]
        return jnp.sum(part, axis=-1, keepdims=True).astype(jnp.int32)

    like = jnp.zeros((keys.shape[0], 1), jnp.int32)
    thr = _kth_largest_key(lambda c: row_count(keys >= c), like, topk)
    need = topk - row_count(keys > thr)
    jb = _tie_index_bound(lambda c: row_count((keys == thr) & (col < c)), need, like, idx_bits)
    thr_ref[...] = thr
    jb_ref[...] = jb


def _select_decode(scores, topk):
    db, n = scores.shape
    idx_bits = int(math.ceil(math.log2(n))) + 1
    return pl.pallas_call(
        functools.partial(_select_decode_body, topk=topk, idx_bits=idx_bits),
        out_shape=[jax.ShapeDtypeStruct((db, 1), jnp.int32)] * 2,
        compiler_params=pltpu.CompilerParams(vmem_limit_bytes=VMEM_LIMIT),
        name="select_decode",
    )(scores)


def _dsa_decode_body(pt_ref, thr_ref, jb_ref, q_ref, ks_ref, vs_ref, sc_ref, *rest, pp, n_steps):
    k_refs = rest[:pp]
    v_refs = rest[pp:2 * pp]
    o_ref, m_ref, l_ref, acc_ref = rest[2 * pp:]
    b = pl.program_id(0)
    j = pl.program_id(1)
    q = q_ref[0]
    thr = thr_ref[b]
    jb = jb_ref[b]
    scale = DH_B ** -0.5
    group = H_B // KV_B
    rows_per_page = PAGE_SIZE * KV_B

    @pl.when(j == 0)
    def _():
        m_ref[...] = jnp.full_like(m_ref, NEG_BIG)
        l_ref[...] = jnp.zeros_like(l_ref)
        acc_ref[...] = jnp.zeros_like(acc_ref)

    keys = _sort_key(sc_ref[0])
    idx = (lax.broadcasted_iota(jnp.int32, (pp, LANES), 0) + j * pp) * PAGE_SIZE \
        + lax.broadcasted_iota(jnp.int32, (pp, LANES), 1)
    sel = (keys > thr) | ((keys == thr) & (idx < jb))

    @pl.when(j < n_steps)
    def _():
        spread = (lax.broadcasted_iota(jnp.int32, (LANES, rows_per_page), 1) // KV_B
                  == lax.broadcasted_iota(jnp.int32, (LANES, rows_per_page), 0))
        sel2 = jnp.dot(jnp.where(sel, 1.0, 0.0).astype(BF16), jnp.where(spread, 1.0, 0.0).astype(BF16),
                       preferred_element_type=F32) > 0.5
        kv_of_row = lax.broadcasted_iota(jnp.int32, (H_B, rows_per_page), 0) // group
        kv_of_lane = lax.broadcasted_iota(jnp.int32, (H_B, rows_per_page), 1) % KV_B
        same_kv = kv_of_row == kv_of_lane
        s_all, ok_all = [], []
        for p in range(pp):
            kp = k_refs[p][0].astype(BF16)
            s = lax.dot_general(q, kp, NT_DIMS, preferred_element_type=F32) * scale
            ok = same_kv & sel2[p:p + 1, :]
            s_all.append(jnp.where(ok, s, NEG_BIG))
            ok_all.append(ok)
        m = m_ref[...]
        m_new = m
        for s in s_all:
            m_new = jnp.maximum(m_new, jnp.max(s, axis=-1, keepdims=True))
        alpha = jnp.exp(m - m_new)
        l = alpha * l_ref[...]
        acc = alpha * acc_ref[...]
        for p in range(pp):
            pr = jnp.where(ok_all[p], jnp.exp(s_all[p] - m_new), 0.0)
            l = l + jnp.sum(pr, axis=-1, keepdims=True)
            acc = acc + jnp.dot(pr.astype(BF16), v_refs[p][0].astype(BF16), preferred_element_type=F32)
        m_ref[...] = m_new
        l_ref[...] = l
        acc_ref[...] = acc

    @pl.when(j == n_steps)
    def _():
        row8 = lax.broadcasted_iota(jnp.int32, (H_B, LANES), 0)
        ks = ks_ref[0].astype(BF16).astype(F32)
        vs = vs_ref[0].astype(BF16).astype(F32)
        kexp = jnp.where(row8 < group, ks[0:1], ks[1:2])
        vexp = jnp.where(row8 < group, vs[0:1], vs[1:2])
        s = jnp.sum(q.astype(F32) * kexp, axis=-1, keepdims=True) * scale
        ok = sel[0:1, 0:1]
        s = jnp.where(ok, s, NEG_BIG)
        m = m_ref[...]
        m_new = jnp.maximum(m, s)
        alpha = jnp.exp(m - m_new)
        pr = jnp.where(ok, jnp.exp(s - m_new), 0.0)
        l = alpha * l_ref[...] + pr
        acc = alpha * acc_ref[...] + pr * vexp
        o_ref[0] = (acc / l).astype(o_ref.dtype)


def _dsa_decode(page_table, thr, jb, qb, k_self, v_self, scores3, cache_k, cache_v):
    db, n_pages = page_table.shape
    pp = min(DSA_PAGES_PER_STEP, n_pages)
    assert n_pages % pp == 0 and scores3.shape[1] >= n_pages + pp
    n_steps = n_pages // pp
    rows = PAGE_SIZE * KV_B
    page = lambda p: pl.BlockSpec(
        (1, rows, DH_B),
        lambda b, j, pt, t, jj: (pt[b, jnp.minimum(j * pp + p, n_pages - 1)], 0, 0))
    tok = lambda r: pl.BlockSpec((1, r, DH_B), lambda b, j, pt, t, jj: (b, 0, 0))
    grid_spec = pltpu.PrefetchScalarGridSpec(
        num_scalar_prefetch=3,
        grid=(db, n_steps + 1),
        in_specs=[tok(H_B), tok(KV_B), tok(KV_B),
                  pl.BlockSpec((1, pp, LANES), lambda b, j, pt, t, jj: (b, j, 0))]
                 + [page(p) for p in range(pp)] + [page(p) for p in range(pp)],
        out_specs=tok(H_B),
        scratch_shapes=[pltpu.VMEM((H_B, 1), F32), pltpu.VMEM((H_B, 1), F32),
                        pltpu.VMEM((H_B, DH_B), F32)],
    )
    return pl.pallas_call(
        functools.partial(_dsa_decode_body, pp=pp, n_steps=n_steps),
        grid_spec=grid_spec,
        out_shape=jax.ShapeDtypeStruct((db, H_B, DH_B), BF16),
        compiler_params=_cparams(("parallel", "arbitrary")),
        name="dsa_decode",
    )(page_table, thr, jb, qb, k_self, v_self, scores3, *([cache_k] * pp), *([cache_v] * pp))


def _rope_inv(half):
    return ROPE_THETA ** (-jnp.arange(half, dtype=F32) / half)


def kernel(x_prompt, x_sample, cache_diff_k, cache_diff_v, cache_dsa_k, cache_dsa_v, cache_idx_k,
           page_table, norm_ffn1, w_ffn1_in, w_ffn1_out, norm_mix, w_in, lambda_q1, lambda_k1,
           lambda_q2, lambda_k2, norm_head, w_out, norm_ffn2, w_ffn2_in, w_ffn2_out, norm_final):
    depth = w_in.shape[0]
    assert depth == 1 and x_sample.shape[1] == 1
    bsz, seq, d = x_prompt.shape
    db = x_sample.shape[0]
    n_phys = cache_diff_k.shape[1]
    n_pages = page_table.shape[1]
    past = n_pages * PAGE_SIZE
    lam_init = 0.8 - 0.6 * math.exp(-0.3 * 0)

    w1i, w1o = w_ffn1_in[0].astype(BF16), w_ffn1_out[0].astype(BF16)
    w2i, w2o = w_ffn2_in[0].astype(BF16), w_ffn2_out[0].astype(BF16)
    wi_all = w_in[0].astype(BF16)
    w_main = wi_all[:, :W_MAIN]
    w_k = wi_all[:, W_MAIN:W_MAIN + D_IDX]
    w_kidx2 = jnp.concatenate([w_k, w_k], axis=1)
    w_widx = jnp.pad(wi_all[:, W_MAIN + D_IDX:], ((0, 0), (0, LANES - H_IDX)))
    wo_all = w_out[0].astype(BF16)
    wo_a, wo_b = wo_all[:H_A * DV_A], wo_all[H_A * DV_A:]
    g1, gm, g2 = norm_ffn1[0][None], norm_mix[0][None], norm_ffn2[0][None]
    gh, gf = norm_head[0][None], norm_final[None]
    lam_params = jnp.stack([lambda_q1[0], lambda_k1[0], lambda_q2[0], lambda_k2[0]]).astype(F32)
    inv32, inv64h = _rope_inv(D_A // 2), _rope_inv(DH_B // 2)
    inv64 = jnp.tile(inv32, LANES // (D_A // 2))[None]
    inv128 = jnp.tile(inv64h, LANES // (DH_B // 2))[None]

    def front(x2d, pos_offset, period, prompt):
        h = _ffn(x2d, g1, w1i, w1o, gf, final_norm=False)
        return h, _inproj(h, gm, w_main, w_kidx2, w_widx, inv64, inv128,
                          pos_offset=pos_offset, period=period, prompt=prompt)

    def back(h, oa, ob):
        h = _outproj(h, oa, ob, wo_a, wo_b)
        return _ffn(h, g2, w2i, w2o, gf, final_norm=True)

    mp = bsz * seq
    hp, (qa, ka, kab, va, vat, qb, kb, kbb, vb, vbt, qi, ki, kid, wit) = front(
        x_prompt.reshape(mp, d), 0, seq, True)
    r3 = lambda a: a.reshape(bsz, seq, a.shape[-1])
    oa = _diff_prompt(r3(qa), r3(kab), vat, lam_params, gh.T, lam_init=lam_init)
    ob = _dsa_prompt(r3(qb), r3(qi), wit, r3(kid), r3(kbb), vbt)
    y_prompt = back(hp, oa.reshape(mp, -1), ob.reshape(mp, -1)).reshape(bsz, seq, d)

    hs, (qa_s, ka_s, va_s, qb_s, kb_s, vb_s, qi_s, ki_s, wi_s) = front(
        x_sample.reshape(db, d), past, 1, False)
    oa_s = _diff_decode(
        page_table, qa_s.reshape(db, H_A, DV_A), ka_s.reshape(db, H_A, DV_A),
        va_s.reshape(db, H_A, DV_A), lam_params, gh,
        cache_diff_k.reshape(n_phys, PAGE_SIZE * H_A, DV_A),
        cache_diff_v.reshape(n_phys, PAGE_SIZE * H_A, DV_A), lam_init=lam_init)
    scores3 = _idx_decode(page_table, qi_s.reshape(db, H_IDX, D_IDX), wi_s.reshape(db, H_IDX, 1),
                          ki_s.reshape(db, 1, D_IDX),
                          jnp.swapaxes(cache_idx_k.reshape(n_phys, PAGE_SIZE, D_IDX), 1, 2))
    thr, jb = _select_decode(scores3.reshape(db, -1), min(TOPK, (past + 1) // 4))
    ob_s = _dsa_decode(
        page_table, thr.reshape(db), jb.reshape(db), qb_s.reshape(db, H_B, DH_B),
        kb_s.reshape(db, KV_B, DH_B), vb_s.reshape(db, KV_B, DH_B), scores3,
        cache_dsa_k.reshape(n_phys, PAGE_SIZE * KV_B, DH_B),
        cache_dsa_v.reshape(n_phys, PAGE_SIZE * KV_B, DH_B))
    y_sample = back(hs, oa_s.reshape(db, -1), ob_s.reshape(db, -1)).reshape(db, 1, d)

    p5 = lambda a, h, w: a.reshape(1, bsz, seq, h, w)
    s5 = lambda a, h, w: a.reshape(1, db, 1, h, w)
    return (y_prompt, y_sample,
            p5(ka, H_A, 2 * D_A), p5(va, H_A, DV_A), p5(kb, KV_B, DH_B), p5(vb, KV_B, DH_B),
            ki.reshape(1, bsz, seq, D_IDX),
            s5(ka_s, H_A, 2 * D_A), s5(va_s, H_A, DV_A), s5(kb_s, KV_B, DH_B), s5(vb_s, KV_B, DH_B),
            ki_s.reshape(1, db, 1, D_IDX))
```
